```python
import math
import jax, jax.numpy as jnp
from jax import lax
import numpy as np

D_MODEL = 4096
BATCH = 1
SEQ = 16384
DEPTH = 1
DEC_BATCH = 4
DEC_SEQ = 4096
PAST_LEN = 128

RET_HEADS = D_MODEL // 512
RET_QK_DIM = 128
RET_V_DIM = 256
RET_CHUNK = 128
RET_ROT_BASE = 10000.0
DIFF_HEADS = D_MODEL // 512
DIFF_QK_DIM = 128
DIFF_V_DIM = 256
DIFF_ROT_DIM = DIFF_QK_DIM // 4
ROPE_THETA = 500000.0
Q_BLOCK = 128
D_FF = ((8 * D_MODEL // 3 + 255) // 256) * 256
DEEPNORM_ALPHA = (2.0 * DEPTH) ** 0.25
DEEPNORM_BETA = (8.0 * DEPTH) ** -0.25
LN_EPS = 1e-5
GN_EPS = 1e-6
RMS_EPS = 1e-5

RQ_COLS = RET_HEADS * RET_QK_DIM
RK_COLS = RET_HEADS * RET_QK_DIM
RV_COLS = RET_HEADS * RET_V_DIM
RG_COLS = RET_HEADS * RET_V_DIM
DQ_COLS = DIFF_HEADS * 2 * DIFF_QK_DIM
DK_COLS = DIFF_HEADS * 2 * DIFF_QK_DIM
DV_COLS = DIFF_HEADS * DIFF_V_DIM
IN_COLS = RQ_COLS + RK_COLS + RV_COLS + RG_COLS + DQ_COLS + DK_COLS + DV_COLS
MIX_WIDTH = RV_COLS + DV_COLS

kernel_name = 'hybrid_retention_diffattn_deepnorm_encoder'


def layer_norm(x, g, b):
    xf = x.astype(jnp.float32)
    mu = jnp.mean(xf, axis=-1, keepdims=True)
    var = jnp.mean(jnp.square(xf - mu), axis=-1, keepdims=True)
    return ((xf - mu) * lax.rsqrt(var + LN_EPS) * g.astype(jnp.float32) + b.astype(jnp.float32)).astype(x.dtype)


def apply_rope(x, pos, rot_dim, base):
    inv_freq = base ** (-jnp.arange(0, rot_dim, 2, dtype=jnp.float32) / rot_dim)
    ang = pos[:, None] * inv_freq[None, :]
    cos = jnp.concatenate([jnp.cos(ang), jnp.cos(ang)], axis=-1)
    sin = jnp.concatenate([jnp.sin(ang), jnp.sin(ang)], axis=-1)
    half = rot_dim // 2
    xr = x[..., :rot_dim].astype(jnp.float32)
    rot = jnp.concatenate([-xr[..., half:], xr[..., :half]], axis=-1)
    out = (xr * cos + rot * sin).astype(x.dtype)
    return jnp.concatenate([out, x[..., rot_dim:]], axis=-1)


def retention_scan(q, k, v, log_gamma):
    b, h, s, _ = q.shape
    dk, dv = q.shape[-1], v.shape[-1]
    c = RET_CHUNK
    n_chunks = s // c
    idx = jnp.arange(c, dtype=jnp.float32)
    rel = idx[:, None] - idx[None, :]
    lg = log_gamma[:, None, None]
    d_intra = jnp.where(rel >= 0, jnp.exp(jnp.maximum(rel, 0.0) * lg), 0.0)
    xi = jnp.exp((idx[None, :] + 1.0) * log_gamma[:, None])[:, :, None]
    zeta = jnp.exp((c - 1.0 - idx[None, :]) * log_gamma[:, None])[:, :, None]
    chunk_decay = jnp.exp(c * log_gamma)[:, None, None]

    def to_chunks(t):
        return t.reshape(b, h, n_chunks, c, t.shape[-1]).transpose(2, 0, 1, 3, 4)

    def step(state, inp):
        qc, kc, vc = inp
        qf, kf, vf = qc.astype(jnp.float32), kc.astype(jnp.float32), vc.astype(jnp.float32)
        scores = jnp.einsum('bhnd,bhmd->bhnm', qf, kf) * d_intra
        o = jnp.einsum('bhnm,bhme->bhne', scores, vf) + jnp.einsum('bhnd,bhde->bhne', qf * xi, state)
        state = chunk_decay * state + jnp.einsum('bhmd,bhme->bhde', kf * zeta, vf)
        return state, o

    state0 = jnp.zeros((b, h, dk, dv), jnp.float32)
    _, o = lax.scan(step, state0, (to_chunks(q), to_chunks(k), to_chunks(v)))
    return o.transpose(1, 2, 0, 3, 4).reshape(b, h, s, dv)


def diff_attention(q, k, v, lam):
    b, h2, s, dq = q.shape
    h = h2 // 2
    n_blocks = s // Q_BLOCK
    scale = dq ** -0.5
    qb = q.reshape(b, h2, n_blocks, Q_BLOCK, dq).transpose(2, 0, 1, 3, 4)

    def block(q_blk):
        sc = jnp.einsum('bhqd,bhkd->bhqk', q_blk, k).astype(jnp.float32) * scale
        a = jax.nn.softmax(sc, axis=-1).reshape(b, h, 2, Q_BLOCK, s)
        a = a[:, :, 0] - lam * a[:, :, 1]
        return jnp.einsum('bhqk,bhkd->bhqd', a.astype(v.dtype), v)

    o = lax.map(block, qb)
    return o.transpose(1, 2, 0, 3, 4).reshape(b, h, s, v.shape[-1])


def encoder_layer(x, w_in, dec_f, dec_b, ret_gn_w, lq1, lk1, lq2, lk2, subln_w, w_out,
                  ln1_g, ln1_b, w_gate, w_up, w_down, ln2_g, ln2_b, lambda_init):
    b, s, _ = x.shape
    pos = jnp.arange(s, dtype=jnp.float32)
    proj = x @ w_in
    splits = [RQ_COLS, RQ_COLS + RK_COLS, RQ_COLS + RK_COLS + RV_COLS,
              RQ_COLS + RK_COLS + RV_COLS + RG_COLS,
              RQ_COLS + RK_COLS + RV_COLS + RG_COLS + DQ_COLS,
              RQ_COLS + RK_COLS + RV_COLS + RG_COLS + DQ_COLS + DK_COLS]
    rq, rk, rv, rg, dq, dk, dv = jnp.split(proj, splits, axis=-1)

    def heads(t, n):
        return t.reshape(b, s, n, -1).transpose(0, 2, 1, 3)

    rq = apply_rope(heads(rq, RET_HEADS), pos, RET_QK_DIM, RET_ROT_BASE)
    rk = apply_rope(heads(rk, RET_HEADS), pos, RET_QK_DIM, RET_ROT_BASE) * (RET_QK_DIM ** -0.5)
    rv = heads(rv, RET_HEADS)
    lg_f = -jnp.exp(dec_f.astype(jnp.float32))
    lg_b = -jnp.exp(dec_b.astype(jnp.float32))
    flip = lambda t: jnp.flip(t, axis=2)
    ro = retention_scan(rq, rk, rv, lg_f) + flip(retention_scan(flip(rq), flip(rk), flip(rv), lg_b))
    mu = jnp.mean(ro, axis=-1, keepdims=True)
    var = jnp.mean(jnp.square(ro - mu), axis=-1, keepdims=True)
    ro = (ro - mu) * lax.rsqrt(var + GN_EPS)
    ro = ro.transpose(0, 2, 1, 3).reshape(b, s, RV_COLS) * ret_gn_w.astype(jnp.float32)
    ro = (jax.nn.silu(rg.astype(jnp.float32)) * ro).astype(x.dtype)

    dq = apply_rope(heads(dq, 2 * DIFF_HEADS), pos, DIFF_ROT_DIM, ROPE_THETA)
    dk = apply_rope(heads(dk, 2 * DIFF_HEADS), pos, DIFF_ROT_DIM, ROPE_THETA)
    dv = heads(dv, DIFF_HEADS)
    lam = (jnp.exp(jnp.sum(lq1.astype(jnp.float32) * lk1.astype(jnp.float32)))
           - jnp.exp(jnp.sum(lq2.astype(jnp.float32) * lk2.astype(jnp.float32))) + lambda_init)
    do = diff_attention(dq, dk, dv, lam).astype(jnp.float32)
    do = do * lax.rsqrt(jnp.mean(jnp.square(do), axis=-1, keepdims=True) + RMS_EPS)
    do = do * subln_w.astype(jnp.float32) * (1.0 - lambda_init)
    do = do.transpose(0, 2, 1, 3).reshape(b, s, DV_COLS).astype(x.dtype)

    mix = jnp.concatenate([ro, do], axis=-1) @ w_out
    x = layer_norm(DEEPNORM_ALPHA * x + mix, ln1_g, ln1_b)
    ffn = (jax.nn.silu(x @ w_gate) * (x @ w_up)) @ w_down
    return layer_norm(DEEPNORM_ALPHA * x + ffn, ln2_g, ln2_b)


def setup_inputs(seed: int = 0) -> dict:
    key = jax.random.key(seed)
    ks = jax.random.split(key, 20)
    f32 = jnp.float32
    nrm = lambda k, shape: jax.random.normal(k, shape, f32)
    x_prompt = nrm(ks[0], (BATCH, SEQ, D_MODEL))
    x_sample = nrm(ks[1], (DEC_BATCH, DEC_SEQ, D_MODEL))
    col_scale = np.ones((IN_COLS,), np.float32)
    v_r0 = RQ_COLS + RK_COLS
    col_scale[v_r0:v_r0 + RV_COLS] = DEEPNORM_BETA
    col_scale[IN_COLS - DV_COLS:] = DEEPNORM_BETA
    w_in = nrm(ks[2], (DEPTH, D_MODEL, IN_COLS)) * (D_MODEL ** -0.5) * jnp.asarray(col_scale)
    base_decay = np.log(-np.log(1.0 - 2.0 ** (-5.0 - np.arange(RET_HEADS)))).astype(np.float32)
    ret_decay_f = jnp.asarray(base_decay)[None, :] + 0.05 * nrm(ks[3], (DEPTH, RET_HEADS))
    ret_decay_b = jnp.asarray(base_decay)[None, :] + 0.05 * nrm(ks[4], (DEPTH, RET_HEADS))
    ret_gn_w = 1.0 + 0.02 * nrm(ks[5], (DEPTH, RV_COLS))
    diff_lambda_q1 = 0.1 * nrm(ks[6], (DEPTH, DIFF_QK_DIM))
    diff_lambda_k1 = 0.1 * nrm(ks[7], (DEPTH, DIFF_QK_DIM))
    diff_lambda_q2 = 0.1 * nrm(ks[8], (DEPTH, DIFF_QK_DIM))
    diff_lambda_k2 = 0.1 * nrm(ks[9], (DEPTH, DIFF_QK_DIM))
    diff_subln_w = 1.0 + 0.02 * nrm(ks[10], (DEPTH, DIFF_V_DIM))
    w_out = nrm(ks[11], (DEPTH, MIX_WIDTH, D_MODEL)) * (MIX_WIDTH ** -0.5) * DEEPNORM_BETA
    ln1_g = 1.0 + 0.02 * nrm(ks[12], (DEPTH, D_MODEL))
    ln1_b = 0.02 * nrm(ks[13], (DEPTH, D_MODEL))
    w_gate = nrm(ks[14], (DEPTH, D_MODEL, D_FF)) * (D_MODEL ** -0.5)
    w_up = nrm(ks[15], (DEPTH, D_MODEL, D_FF)) * (D_MODEL ** -0.5) * DEEPNORM_BETA
    w_down = nrm(ks[16], (DEPTH, D_FF, D_MODEL)) * (D_FF ** -0.5) * DEEPNORM_BETA
    ln2_g = 1.0 + 0.02 * nrm(ks[17], (DEPTH, D_MODEL))
    ln2_b = 0.02 * nrm(ks[18], (DEPTH, D_MODEL))
    return {'x_prompt': x_prompt, 'x_sample': x_sample, 'w_in': w_in,
            'ret_decay_f': ret_decay_f, 'ret_decay_b': ret_decay_b, 'ret_gn_w': ret_gn_w,
            'diff_lambda_q1': diff_lambda_q1, 'diff_lambda_k1': diff_lambda_k1,
            'diff_lambda_q2': diff_lambda_q2, 'diff_lambda_k2': diff_lambda_k2,
            'diff_subln_w': diff_subln_w, 'w_out': w_out, 'ln1_g': ln1_g, 'ln1_b': ln1_b,
            'w_gate': w_gate, 'w_up': w_up, 'w_down': w_down, 'ln2_g': ln2_g, 'ln2_b': ln2_b}


def reference(x_prompt, x_sample, w_in, ret_decay_f, ret_decay_b, ret_gn_w,
              diff_lambda_q1, diff_lambda_k1, diff_lambda_q2, diff_lambda_k2, diff_subln_w,
              w_out, ln1_g, ln1_b, w_gate, w_up, w_down, ln2_g, ln2_b):
    y_prompt = x_prompt
    y_sample = x_sample
    for l in range(DEPTH):
        lambda_init = 0.8 - 0.6 * math.exp(-0.3 * l)
        params = (w_in[l], ret_decay_f[l], ret_decay_b[l], ret_gn_w[l],
                  diff_lambda_q1[l], diff_lambda_k1[l], diff_lambda_q2[l], diff_lambda_k2[l],
                  diff_subln_w[l], w_out[l], ln1_g[l], ln1_b[l],
                  w_gate[l], w_up[l], w_down[l], ln2_g[l], ln2_b[l])
        y_prompt = encoder_layer(y_prompt, *params, lambda_init)
        y_sample = encoder_layer(y_sample, *params, lambda_init)
    return (y_prompt, y_sample)
```

```python
import functools
import math

import jax
import jax.numpy as jnp
from jax import lax
from jax.experimental import pallas as pl
from jax.experimental.pallas import tpu as pltpu

D_MODEL = 4096
DEPTH = 1
RET_HEADS = 8
RET_QK_DIM = 128
RET_V_DIM = 256
RET_ROT_BASE = 10000.0
DIFF_HEADS = 8
DIFF_QK_DIM = 128
DIFF_V_DIM = 256
DIFF_ROT_DIM = DIFF_QK_DIM // 4
ROPE_THETA = 500000.0
D_FF = 11008
DEEPNORM_ALPHA = (2.0 * DEPTH) ** 0.25
LN_EPS = 1e-5
GN_EPS = 1e-6
RMS_EPS = 1e-5

RQ_COLS = RET_HEADS * RET_QK_DIM
RV_COLS = RET_HEADS * RET_V_DIM
DQ_COLS = DIFF_HEADS * 2 * DIFF_QK_DIM
DV_COLS = DIFF_HEADS * DIFF_V_DIM

LANES = 128
VMEM_LIMIT_BYTES = 56 * 1024 * 1024
D_FF_PAD = 11264
RET_CHUNK = 256
DOWN_COL_CHUNK = 512
LN_ROW_CHUNK = 64

BF16 = jnp.bfloat16
F32 = jnp.float32


def _tile(n, pref):
    t = pref
    while t > 8 and n % t:
        t //= 2
    assert n % t == 0, (n, pref)
    return t


def _params(sem):
    return pltpu.CompilerParams(dimension_semantics=sem, vmem_limit_bytes=VMEM_LIMIT_BYTES)


def _proj_kernel(x_ref, w_ref, *rest, mode, hw):
    if mode == "plain":
        (o_ref,) = rest
        tab_ref = None
    else:
        tab_ref, o_ref = rest
    acc = jnp.dot(x_ref[...], w_ref[...], preferred_element_type=F32)
    for c in range(o_ref.shape[0]):
        blk = acc[:, c * hw:(c + 1) * hw]
        if mode == "rope_full":
            cos = tab_ref[0, :, 0:LANES]
            sin = tab_ref[0, :, LANES:2 * LANES]
            blk = blk * cos + pltpu.roll(blk, LANES // 2, 1) * sin
        elif mode == "rope_partial":
            half = DIFF_ROT_DIM // 2
            a = tab_ref[0, :, 0:LANES]
            b = tab_ref[0, :, LANES:2 * LANES]
            c2 = tab_ref[0, :, 2 * LANES:3 * LANES]
            blk = blk * a + pltpu.roll(blk, half, 1) * b + pltpu.roll(blk, LANES - half, 1) * c2
        o_ref[c] = blk.astype(o_ref.dtype)


def _project(xb, w, seq, mode, hw, tab=None, tab_split=None):
    t, d = xb.shape
    n = w.shape[1]
    tm = _tile(seq, 1024)
    tn = _tile(n, 512)
    grid = (t // tm, n // tn)
    in_specs = [pl.BlockSpec((tm, d), lambda i, j: (i, 0)),
                pl.BlockSpec((d, tn), lambda i, j: (0, j))]
    args = [xb, w]
    if mode != "plain":
        per_seq = seq // tm
        split = tab_split // tn
        in_specs.append(pl.BlockSpec((1, tm, tab.shape[2]),
                                     lambda i, j: (j // split, i % per_seq, 0)))
        args.append(tab)
    return pl.pallas_call(
        functools.partial(_proj_kernel, mode=mode, hw=hw),
        grid=grid,
        in_specs=in_specs,
        out_specs=pl.BlockSpec((tn // hw, tm, hw), lambda i, j: (j, i, 0)),
        out_shape=jax.ShapeDtypeStruct((n // hw, t, hw), BF16),
        compiler_params=_params(("parallel", "arbitrary")),
        name="proj_" + mode,
    )(*args)


def _ret_kernel(decf_ref, decb_ref, q_ref, k_ref, v_ref, g_ref, gnw_ref, o_ref,
                sf_scr, sb_scr, sball_scr, *, nc, chunk):
    t = pl.program_id(2)
    lg_f = -jnp.exp(decf_ref[...])
    lg_b = -jnp.exp(decb_ref[...])
    row = lax.broadcasted_iota(jnp.int32, (chunk, 1), 0).astype(F32)
    k = k_ref[...].astype(F32)
    v = v_ref[...]
    tdot = (((0,), (0,)), ((), ()))

    @pl.when(t == 0)
    def _():
        sb_scr[...] = jnp.zeros_like(sb_scr)

    @pl.when(t < nc)
    def _():
        cb = nc - 1 - t
        sball_scr[cb] = sb_scr[...].astype(BF16)
        zeta_b = jnp.exp(row * lg_b)
        kz = (k * zeta_b).astype(BF16)
        sb_scr[...] = (jnp.exp(chunk * lg_b) * sb_scr[...]
                       + lax.dot_general(kz, v, tdot, preferred_element_type=F32))

    @pl.when(t == nc)
    def _():
        sf_scr[...] = jnp.zeros_like(sf_scr)

    @pl.when(t >= nc)
    def _():
        c = t - nc
        q = q_ref[...]
        rel = (lax.broadcasted_iota(jnp.int32, (chunk, chunk), 0)
               - lax.broadcasted_iota(jnp.int32, (chunk, chunk), 1)).astype(F32)
        dmat = (jnp.where(rel >= 0, jnp.exp(jnp.maximum(rel, 0.0) * lg_f), 0.0)
                + jnp.where(rel <= 0, jnp.exp(jnp.maximum(-rel, 0.0) * lg_b), 0.0))
        s = lax.dot_general(q, k_ref[...], (((1,), (1,)), ((), ())), preferred_element_type=F32)
        a = (s * dmat).astype(BF16)
        qf = q.astype(F32)
        xi_f = jnp.exp((row + 1.0) * lg_f)
        xi_b = jnp.exp((chunk - row) * lg_b)
        o = jnp.dot(a, v, preferred_element_type=F32)
        o += jnp.dot((qf * xi_f).astype(BF16), sf_scr[...].astype(BF16), preferred_element_type=F32)
        o += jnp.dot((qf * xi_b).astype(BF16), sball_scr[c], preferred_element_type=F32)
        zeta_f = jnp.exp((chunk - 1.0 - row) * lg_f)
        kz = (k * zeta_f).astype(BF16)
        sf_scr[...] = (jnp.exp(chunk * lg_f) * sf_scr[...]
                       + lax.dot_general(kz, v, tdot, preferred_element_type=F32))
        mu = jnp.mean(o, axis=-1, keepdims=True)
        var = jnp.mean(jnp.square(o - mu), axis=-1, keepdims=True)
        on = (o - mu) * lax.rsqrt(var + GN_EPS) * gnw_ref[...]
        g = g_ref[...].astype(F32)
        o_ref[...] = (g * (1.0 / (1.0 + jnp.exp(-g))) * on).astype(o_ref.dtype)


def _retention(rqk, plain, dec_f, dec_b, gn_w, batch, seq):
    t = rqk.shape[1]
    chunk = _tile(seq, RET_CHUNK)
    nc = seq // chunk

    def cidx(b, s):
        return b * nc + jnp.where(s < nc, nc - 1 - s, s - nc)

    def oidx(b, s):
        return b * nc + jnp.maximum(s - nc, 0)

    return pl.pallas_call(
        functools.partial(_ret_kernel, nc=nc, chunk=chunk),
        grid=(batch, RET_HEADS, 2 * nc),
        in_specs=[
            pl.BlockSpec((None, 1, 1), lambda b, h, s: (h, 0, 0)),
            pl.BlockSpec((None, 1, 1), lambda b, h, s: (h, 0, 0)),
            pl.BlockSpec((None, chunk, RET_QK_DIM), lambda b, h, s: (h, cidx(b, s), 0)),
            pl.BlockSpec((None, chunk, RET_QK_DIM), lambda b, h, s: (RET_HEADS + h, cidx(b, s), 0)),
            pl.BlockSpec((None, chunk, RET_V_DIM), lambda b, h, s: (h, cidx(b, s), 0)),
            pl.BlockSpec((None, chunk, RET_V_DIM), lambda b, h, s: (RET_HEADS + h, cidx(b, s), 0)),
            pl.BlockSpec((None, 1, RET_V_DIM), lambda b, h, s: (h, 0, 0)),
        ],
        out_specs=pl.BlockSpec((chunk, RET_V_DIM), lambda b, h, s: (oidx(b, s), h)),
        out_shape=jax.ShapeDtypeStruct((t, RV_COLS), BF16),
        scratch_shapes=[
            pltpu.VMEM((RET_QK_DIM, RET_V_DIM), F32),
            pltpu.VMEM((RET_QK_DIM, RET_V_DIM), F32),
            pltpu.VMEM((nc, RET_QK_DIM, RET_V_DIM), BF16),
        ],
        compiler_params=_params(("parallel", "parallel", "arbitrary")),
        name="retention",
    )(dec_f.reshape(RET_HEADS, 1, 1), dec_b.reshape(RET_HEADS, 1, 1),
      rqk, rqk, plain, plain, gn_w.reshape(RET_HEADS, 1, RET_V_DIM))


def _diff_kernel(lq1_ref, lk1_ref, lq2_ref, lk2_ref, sub_ref, q_ref, k_ref, v_ref, o_ref,
                 m_scr, l_scr, acc_scr, *, n_kv, tk, lambda_init):
    m_scr[...] = jnp.full(m_scr.shape, -jnp.inf, F32)
    l_scr[...] = jnp.zeros_like(l_scr)
    acc_scr[...] = jnp.zeros_like(acc_scr)

    def body(j, carry):
        start = pl.multiple_of(j * tk, tk)
        v = v_ref[pl.ds(start, tk), :]
        for a in range(2):
            s = lax.dot_general(q_ref[a], k_ref[a, pl.ds(start, tk), :],
                                (((1,), (1,)), ((), ())), preferred_element_type=F32)
            m_prev = m_scr[a]
            m_new = jnp.maximum(m_prev, jnp.max(s, axis=1, keepdims=True))
            alpha = jnp.exp(m_prev - m_new)
            p = jnp.exp(s - m_new)
            l_scr[a] = alpha * l_scr[a] + jnp.sum(p, axis=1, keepdims=True)
            acc_scr[a] = alpha * acc_scr[a] + jnp.dot(p.astype(BF16), v, preferred_element_type=F32)
            m_scr[a] = m_new
        return carry

    lax.fori_loop(0, n_kv, body, 0)

    lam = (jnp.exp(jnp.sum(lq1_ref[...] * lk1_ref[...], axis=1, keepdims=True))
           - jnp.exp(jnp.sum(lq2_ref[...] * lk2_ref[...], axis=1, keepdims=True)) + lambda_init)
    o = acc_scr[0] * (1.0 / l_scr[0]) - lam * (acc_scr[1] * (1.0 / l_scr[1]))
    o = o * lax.rsqrt(jnp.mean(jnp.square(o), axis=-1, keepdims=True) + RMS_EPS)
    o_ref[...] = (o * sub_ref[...] * (1.0 - lambda_init)).astype(o_ref.dtype)


def _diff_attention(dqk, plain, lq1, lk1, lq2, lk2, subln_w, batch, seq, lambda_init):
    t = dqk.shape[1]
    tq = _tile(seq, 512)
    tk = _tile(seq, 512)
    nq = seq // tq
    vec = lambda a: a.reshape(1, -1).astype(F32)
    vspec = lambda n: pl.BlockSpec((1, n), lambda b, h, i: (0, 0))
    return pl.pallas_call(
        functools.partial(_diff_kernel, n_kv=seq // tk, tk=tk, lambda_init=lambda_init),
        grid=(batch, DIFF_HEADS, nq),
        in_specs=[
            vspec(DIFF_QK_DIM), vspec(DIFF_QK_DIM), vspec(DIFF_QK_DIM), vspec(DIFF_QK_DIM),
            vspec(DIFF_V_DIM),
            pl.BlockSpec((2, tq, DIFF_QK_DIM), lambda b, h, i: (h, b * nq + i, 0)),
            pl.BlockSpec((2, seq, DIFF_QK_DIM), lambda b, h, i: (DIFF_HEADS + h, b, 0)),
            pl.BlockSpec((None, seq, DIFF_V_DIM), lambda b, h, i: (2 * RET_HEADS + h, b, 0)),
        ],
        out_specs=pl.BlockSpec((tq, DIFF_V_DIM), lambda b, h, i: (b * nq + i, h)),
        out_shape=jax.ShapeDtypeStruct((t, DV_COLS), BF16),
        scratch_shapes=[
            pltpu.VMEM((2, tq, 1), F32),
            pltpu.VMEM((2, tq, 1), F32),
            pltpu.VMEM((2, tq, DIFF_V_DIM), F32),
        ],
        compiler_params=_params(("parallel", "parallel", "arbitrary")),
        name="diff_attention",
    )(vec(lq1), vec(lk1), vec(lq2), vec(lk2), vec(subln_w), dqk, dqk, plain)


def _layer_norm_chunks(pre_ref, g_ref, b_ref, outs, n_chunks, tn):
    width = n_chunks * tn
    mu = sum(jnp.sum(pre_ref[c], axis=1, keepdims=True) for c in range(n_chunks)) / width
    var = sum(jnp.sum(jnp.square(pre_ref[c] - mu), axis=1, keepdims=True)
              for c in range(n_chunks)) / width
    rstd = lax.rsqrt(var + LN_EPS)
    for c in range(n_chunks):
        sl = slice(c * tn, (c + 1) * tn)
        y = (pre_ref[c] - mu) * rstd * g_ref[:, sl] + b_ref[:, sl]
        for o_ref in outs:
            o_ref[:, sl] = y.astype(o_ref.dtype)


def _outproj_kernel(ro_ref, do_ref, wt_ref, wb_ref, x_ref, g_ref, b_ref, o32_ref, o16_ref,
                    pre_scr, *, nj, tn):
    j = pl.program_id(1)
    pre_scr[j] = (DEEPNORM_ALPHA * x_ref[...]
                  + jnp.dot(ro_ref[...], wt_ref[...], preferred_element_type=F32)
                  + jnp.dot(do_ref[...], wb_ref[...], preferred_element_type=F32))

    @pl.when(j == nj - 1)
    def _():
        _layer_norm_chunks(pre_scr, g_ref, b_ref, (o32_ref, o16_ref), nj, tn)


def _out_projection(ro, do, w_out, x, ln_g, ln_b):
    t = x.shape[0]
    tm = _tile(t, 256)
    tn = 512
    nj = D_MODEL // tn
    half = RV_COLS
    row = lambda i, j: (i, 0)
    return pl.pallas_call(
        functools.partial(_outproj_kernel, nj=nj, tn=tn),
        grid=(t // tm, nj),
        in_specs=[
            pl.BlockSpec((tm, half), row),
            pl.BlockSpec((tm, half), row),
            pl.BlockSpec((half, tn), lambda i, j: (0, j)),
            pl.BlockSpec((half, tn), lambda i, j: (1, j)),
            pl.BlockSpec((tm, tn), lambda i, j: (i, j)),
            pl.BlockSpec((1, D_MODEL), lambda i, j: (0, 0)),
            pl.BlockSpec((1, D_MODEL), lambda i, j: (0, 0)),
        ],
        out_specs=[pl.BlockSpec((tm, D_MODEL), row), pl.BlockSpec((tm, D_MODEL), row)],
        out_shape=[jax.ShapeDtypeStruct((t, D_MODEL), F32),
                   jax.ShapeDtypeStruct((t, D_MODEL), BF16)],
        scratch_shapes=[pltpu.VMEM((nj, tm, tn), F32)],
        compiler_params=_params(("parallel", "arbitrary")),
        name="out_projection",
    )(ro, do, w_out, w_out, x, ln_g.reshape(1, -1), ln_b.reshape(1, -1))


def _gate_up_kernel(x_ref, wg_ref, wu_ref, o_ref):
    x = x_ref[...]
    g = jnp.dot(x, wg_ref[...], preferred_element_type=F32)
    u = jnp.dot(x, wu_ref[...], preferred_element_type=F32)
    o_ref[...] = (g * (1.0 / (1.0 + jnp.exp(-g))) * u).astype(o_ref.dtype)


def _gate_up(xb, wg, wu):
    t, d = xb.shape
    n = wg.shape[1]
    tm = _tile(t, 1024)
    tn = 512
    return pl.pallas_call(
        _gate_up_kernel,
        grid=(t // tm, n // tn),
        in_specs=[pl.BlockSpec((tm, d), lambda i, j: (i, 0)),
                  pl.BlockSpec((d, tn), lambda i, j: (0, j)),
                  pl.BlockSpec((d, tn), lambda i, j: (0, j))],
        out_specs=pl.BlockSpec((tm, tn), lambda i, j: (i, j)),
        out_shape=jax.ShapeDtypeStruct((t, n), BF16),
        compiler_params=_params(("parallel", "arbitrary")),
        name="ffn_gate_up",
    )(xb, wg, wu)


def _down_kernel(h_ref, w_ref, x_ref, g_ref, b_ref, o_ref, *, nk):
    k = pl.program_id(1)
    tm, width = o_ref.shape

    @pl.when(k == 0)
    def _():
        o_ref[...] = DEEPNORM_ALPHA * x_ref[...]

    h = h_ref[...]
    for c in range(width // DOWN_COL_CHUNK):
        sl = slice(c * DOWN_COL_CHUNK, (c + 1) * DOWN_COL_CHUNK)
        o_ref[:, sl] += jnp.dot(h, w_ref[:, sl], preferred_element_type=F32)

    @pl.when(k == nk - 1)
    def _():
        def rows(r, carry):
            rs = pl.ds(pl.multiple_of(r * LN_ROW_CHUNK, LN_ROW_CHUNK), LN_ROW_CHUNK)
            pre = o_ref[rs, :]
            mu = jnp.mean(pre, axis=-1, keepdims=True)
            var = jnp.mean(jnp.square(pre - mu), axis=-1, keepdims=True)
            o_ref[rs, :] = (pre - mu) * lax.rsqrt(var + LN_EPS) * g_ref[...] + b_ref[...]
            return carry

        lax.fori_loop(0, tm // LN_ROW_CHUNK, rows, 0)


def _down_projection(h, wd, x1, ln_g, ln_b):
    t, kdim = h.shape
    tm = _tile(t, 512)
    tk = 1024
    nk = kdim // tk
    return pl.pallas_call(
        functools.partial(_down_kernel, nk=nk),
        grid=(t // tm, nk),
        in_specs=[pl.BlockSpec((tm, tk), lambda i, k: (i, k)),
                  pl.BlockSpec((tk, D_MODEL), lambda i, k: (k, 0)),
                  pl.BlockSpec((tm, D_MODEL), lambda i, k: (i, 0)),
                  pl.BlockSpec((1, D_MODEL), lambda i, k: (0, 0)),
                  pl.BlockSpec((1, D_MODEL), lambda i, k: (0, 0))],
        out_specs=pl.BlockSpec((tm, D_MODEL), lambda i, k: (i, 0)),
        out_shape=jax.ShapeDtypeStruct((t, D_MODEL), F32),
        compiler_params=_params(("parallel", "arbitrary")),
        name="ffn_down",
    )(h, wd, x1, ln_g.reshape(1, -1), ln_b.reshape(1, -1))


def _rope_tables(seq):
    pos = jnp.arange(seq, dtype=F32)

    def cos_sin(rot_dim, base):
        inv_freq = base ** (-jnp.arange(0, rot_dim, 2, dtype=F32) / rot_dim)
        ang = pos[:, None] * inv_freq[None, :]
        return jnp.cos(ang), jnp.sin(ang)

    c, s = cos_sin(RET_QK_DIM, RET_ROT_BASE)
    ret = jnp.concatenate([c, c, -s, s], axis=-1)
    ret_tab = jnp.stack([ret, ret * (RET_QK_DIM ** -0.5)])

    c, s = cos_sin(DIFF_ROT_DIM, ROPE_THETA)
    half = DIFF_ROT_DIM // 2
    rest = DIFF_QK_DIM - DIFF_ROT_DIM
    ones = jnp.ones((seq, rest), F32)
    zeros = jnp.zeros((seq, rest), F32)
    zh = jnp.zeros((seq, half), F32)
    a = jnp.concatenate([c, c, ones], axis=-1)
    b = jnp.concatenate([zh, s, zeros], axis=-1)
    c2 = jnp.concatenate([-s, zh, zeros], axis=-1)
    diff = jnp.concatenate([a, b, c2], axis=-1)
    diff_tab = jnp.stack([diff * (DIFF_QK_DIM ** -0.5), diff])
    return ret_tab, diff_tab


def _prepare_weights(w_in, w_out, w_gate, w_up, w_down):
    wb = w_in.astype(BF16)
    c0 = 2 * RQ_COLS
    c1 = c0 + 2 * RV_COLS
    c2 = c1 + 2 * DQ_COLS
    w_rqk = wb[:, :c0]
    w_plain = jnp.concatenate([wb[:, c0:c1], wb[:, c2:]], axis=1)
    w_dqk = wb[:, c1:c2]
    pad = D_FF_PAD - D_FF
    wg = jnp.pad(w_gate.astype(BF16), ((0, 0), (0, pad)))
    wu = jnp.pad(w_up.astype(BF16), ((0, 0), (0, pad)))
    wd = jnp.pad(w_down.astype(BF16), ((0, pad), (0, 0)))
    return w_rqk, w_plain, w_dqk, w_out.astype(BF16), wg, wu, wd


def _encoder_layer(x, weights, dec_f, dec_b, gn_w, lq1, lk1, lq2, lk2, subln_w,
                   ln1_g, ln1_b, ln2_g, ln2_b, lambda_init):
    batch, seq, d = x.shape
    w_rqk, w_plain, w_dqk, w_out, wg, wu, wd = weights
    x2 = x.reshape(batch * seq, d)
    xb = x2.astype(BF16)
    ret_tab, diff_tab = _rope_tables(seq)

    rqk = _project(xb, w_rqk, seq, "rope_full", RET_QK_DIM, ret_tab, RQ_COLS)
    plain = _project(xb, w_plain, seq, "plain", RET_V_DIM)
    dqk = _project(xb, w_dqk, seq, "rope_partial", DIFF_QK_DIM, diff_tab, DQ_COLS)

    ro = _retention(rqk, plain, dec_f, dec_b, gn_w, batch, seq)
    do = _diff_attention(dqk, plain, lq1, lk1, lq2, lk2, subln_w, batch, seq, lambda_init)

    x1, x1b = _out_projection(ro, do, w_out, x2, ln1_g, ln1_b)
    h = _gate_up(x1b, wg, wu)
    y = _down_projection(h, wd, x1, ln2_g, ln2_b)
    return y.reshape(batch, seq, d)


def kernel(x_prompt, x_sample, w_in, ret_decay_f, ret_decay_b, ret_gn_w, diff_lambda_q1,
           diff_lambda_k1, diff_lambda_q2, diff_lambda_k2, diff_subln_w, w_out, ln1_g, ln1_b,
           w_gate, w_up, w_down, ln2_g, ln2_b):
    y_prompt, y_sample = x_prompt, x_sample
    for l in range(DEPTH):
        lambda_init = 0.8 - 0.6 * math.exp(-0.3 * l)
        weights = _prepare_weights(w_in[l], w_out[l], w_gate[l], w_up[l], w_down[l])
        rest = (ret_decay_f[l], ret_decay_b[l], ret_gn_w[l], diff_lambda_q1[l], diff_lambda_k1[l],
                diff_lambda_q2[l], diff_lambda_k2[l], diff_subln_w[l], ln1_g[l], ln1_b[l],
                ln2_g[l], ln2_b[l], lambda_init)
        y_prompt = _encoder_layer(y_prompt, weights, *rest)
        y_sample = _encoder_layer(y_sample, weights, *rest)
    return (y_prompt, y_sample)
```

```python
import functools
import math

import jax
import jax.numpy as jnp
from jax import lax
from jax.experimental import pallas as pl
from jax.experimental.pallas import tpu as pltpu

D_MODEL = 4096
DEPTH = 1
RET_HEADS = 8
RET_QK_DIM = 128
RET_V_DIM = 256
RET_ROT_BASE = 10000.0
DIFF_HEADS = 8
DIFF_QK_DIM = 128
DIFF_V_DIM = 256
DIFF_ROT_DIM = DIFF_QK_DIM // 4
ROPE_THETA = 500000.0
D_FF = 11008
DEEPNORM_ALPHA = (2.0 * DEPTH) ** 0.25
LN_EPS = 1e-5
GN_EPS = 1e-6
RMS_EPS = 1e-5

RQ_COLS = RET_HEADS * RET_QK_DIM
RV_COLS = RET_HEADS * RET_V_DIM
DQ_COLS = DIFF_HEADS * 2 * DIFF_QK_DIM
DV_COLS = DIFF_HEADS * DIFF_V_DIM

LANES = 128
VMEM_LIMIT_BYTES = 56 * 1024 * 1024
D_FF_PAD = 11264
PROJ_ROWS = 1024
RET_CHUNK = 256
RET_BLOCK = 1024
ATTN_Q_ROWS = 512
DOWN_COL_CHUNK = 512
LN_ROW_CHUNK = 64
LOG2_E = math.log2(math.e)

BF16 = jnp.bfloat16
F32 = jnp.float32

NT_DIMS = (((1,), (1,)), ((), ()))
TN_DIMS = (((0,), (0,)), ((), ()))


def _tile(n, pref):
    t = pref
    while t > 8 and n % t:
        t //= 2
    assert n % t == 0, (n, pref)
    return t


def _params(sem):
    return pltpu.CompilerParams(dimension_semantics=sem, vmem_limit_bytes=VMEM_LIMIT_BYTES)


def _swish(g):
    return g * (1.0 / (1.0 + jnp.exp(-g)))


def _layer_norm_rows(src_ref, g_ref, b_ref, outs):
    def rows(r, carry):
        rs = pl.ds(pl.multiple_of(r * LN_ROW_CHUNK, LN_ROW_CHUNK), LN_ROW_CHUNK)
        pre = src_ref[rs, :]
        mu = jnp.mean(pre, axis=-1, keepdims=True)
        var = jnp.mean(jnp.square(pre - mu), axis=-1, keepdims=True)
        y = (pre - mu) * lax.rsqrt(var + LN_EPS) * g_ref[...] + b_ref[...]
        for o_ref in outs:
            o_ref[rs, :] = y.astype(o_ref.dtype)
        return carry

    lax.fori_loop(0, src_ref.shape[0] // LN_ROW_CHUNK, rows, 0)


def _proj_kernel(x_ref, w_ref, *rest, mode, hw):
    if mode in ("plain", "plain_t"):
        (o_ref,) = rest
        tab_ref = None
    else:
        tab_ref, o_ref = rest
    acc = jnp.dot(x_ref[...], w_ref[...], preferred_element_type=F32)
    for c in range(o_ref.shape[0]):
        blk = acc[:, c * hw:(c + 1) * hw]
        if mode == "rope_full":
            cos = tab_ref[0, :, 0:LANES]
            sin = tab_ref[0, :, LANES:2 * LANES]
            blk = blk * cos + pltpu.roll(blk, LANES // 2, 1) * sin
        elif mode == "rope_partial":
            half = DIFF_ROT_DIM // 2
            a = tab_ref[0, :, 0:LANES]
            b = tab_ref[0, :, LANES:2 * LANES]
            c2 = tab_ref[0, :, 2 * LANES:3 * LANES]
            blk = blk * a + pltpu.roll(blk, half, 1) * b + pltpu.roll(blk, LANES - half, 1) * c2
        elif mode == "plain_t":
            blk = blk.T
        o_ref[c] = blk.astype(o_ref.dtype)


def _project(xb, w, seq, mode, hw, tab=None, tab_split=None):
    t, d = xb.shape
    n = w.shape[1]
    tm = _tile(seq, PROJ_ROWS)
    tn = _tile(n, 512)
    grid = (t // tm, n // tn)
    in_specs = [pl.BlockSpec((tm, d), lambda i, j: (i, 0)),
                pl.BlockSpec((d, tn), lambda i, j: (0, j))]
    args = [xb, w]
    if tab is not None:
        per_seq = seq // tm
        split = tab_split // tn
        in_specs.append(pl.BlockSpec((1, tm, tab.shape[2]),
                                     lambda i, j: (j // split, i % per_seq, 0)))
        args.append(tab)
    if mode == "plain_t":
        out_spec = pl.BlockSpec((tn // hw, None, hw, tm), lambda i, j: (j, i, 0, 0))
        out_shape = jax.ShapeDtypeStruct((n // hw, t // tm, hw, tm), BF16)
    else:
        out_spec = pl.BlockSpec((tn // hw, tm, hw), lambda i, j: (j, i, 0))
        out_shape = jax.ShapeDtypeStruct((n // hw, t, hw), BF16)
    return pl.pallas_call(
        functools.partial(_proj_kernel, mode=mode, hw=hw),
        grid=grid,
        in_specs=in_specs,
        out_specs=out_spec,
        out_shape=out_shape,
        compiler_params=_params(("parallel", "arbitrary")),
        name="proj_" + mode,
    )(*args)


def _ret_kernel(decf_ref, decb_ref, q_ref, k_ref, v_ref, g_ref, gnw_ref, o_ref,
                sf_scr, sb_scr, sball_scr, *, nblk, nb, chunk):
    s = pl.program_id(2)
    lg_f = -jnp.exp(decf_ref[...])
    lg_b = -jnp.exp(decb_ref[...])
    row = lax.broadcasted_iota(jnp.int32, (chunk, 1), 0).astype(F32)

    @pl.when(s == 0)
    def _():
        sb_scr[...] = jnp.zeros_like(sb_scr)

    @pl.when(s < nblk)
    def _():
        blk = nblk - 1 - s
        zeta_b = jnp.exp(row * lg_b)
        decay_b = jnp.exp(chunk * lg_b)
        state = sb_scr[...]
        for ci in reversed(range(nb)):
            rs = slice(ci * chunk, (ci + 1) * chunk)
            sball_scr[blk * nb + ci] = state.astype(BF16)
            kz = (k_ref[rs, :].astype(F32) * zeta_b).astype(BF16)
            state = decay_b * state + lax.dot_general(kz, v_ref[rs, :], TN_DIMS,
                                                      preferred_element_type=F32)
        sb_scr[...] = state

    @pl.when(s == nblk)
    def _():
        sf_scr[...] = jnp.zeros_like(sf_scr)

    @pl.when(s >= nblk)
    def _():
        blk = s - nblk
        rel = (lax.broadcasted_iota(jnp.int32, (chunk, chunk), 0)
               - lax.broadcasted_iota(jnp.int32, (chunk, chunk), 1)).astype(F32)
        dmat = (jnp.where(rel >= 0, jnp.exp(jnp.maximum(rel, 0.0) * lg_f), 0.0)
                + jnp.where(rel <= 0, jnp.exp(jnp.maximum(-rel, 0.0) * lg_b), 0.0))
        xi_f = jnp.exp((row + 1.0) * lg_f)
        xi_b = jnp.exp((chunk - row) * lg_b)
        zeta_f = jnp.exp((chunk - 1.0 - row) * lg_f)
        decay_f = jnp.exp(chunk * lg_f)
        gnw = gnw_ref[...]
        state = sf_scr[...]
        for ci in range(nb):
            rs = slice(ci * chunk, (ci + 1) * chunk)
            q = q_ref[rs, :]
            k = k_ref[rs, :]
            v = v_ref[rs, :]
            sc = lax.dot_general(q, k, NT_DIMS, preferred_element_type=F32)
            qf = q.astype(F32)
            o = jnp.dot((sc * dmat).astype(BF16), v, preferred_element_type=F32)
            o += jnp.dot((qf * xi_f).astype(BF16), state.astype(BF16), preferred_element_type=F32)
            o += jnp.dot((qf * xi_b).astype(BF16), sball_scr[blk * nb + ci],
                         preferred_element_type=F32)
            kz = (k.astype(F32) * zeta_f).astype(BF16)
            state = decay_f * state + lax.dot_general(kz, v, TN_DIMS, preferred_element_type=F32)
            mu = jnp.mean(o, axis=-1, keepdims=True)
            var = jnp.mean(jnp.square(o - mu), axis=-1, keepdims=True)
            on = (o - mu) * lax.rsqrt(var + GN_EPS) * gnw
            o_ref[rs, :] = (_swish(g_ref[rs, :].astype(F32)) * on).astype(o_ref.dtype)
        sf_scr[...] = state


def _retention(rqk, rvg, dec_f, dec_b, gn_w, batch, seq):
    t = rqk.shape[1]
    rows = _tile(seq, RET_BLOCK)
    chunk = _tile(rows, RET_CHUNK)
    nb = rows // chunk
    nblk = seq // rows

    def bidx(b, s):
        return b * nblk + jnp.where(s < nblk, nblk - 1 - s, s - nblk)

    def oidx(b, s):
        return b * nblk + jnp.maximum(s - nblk, 0)

    return pl.pallas_call(
        functools.partial(_ret_kernel, nblk=nblk, nb=nb, chunk=chunk),
        grid=(batch, RET_HEADS, 2 * nblk),
        in_specs=[
            pl.BlockSpec((None, 1, 1), lambda b, h, s: (h, 0, 0)),
            pl.BlockSpec((None, 1, 1), lambda b, h, s: (h, 0, 0)),
            pl.BlockSpec((None, rows, RET_QK_DIM), lambda b, h, s: (h, bidx(b, s), 0)),
            pl.BlockSpec((None, rows, RET_QK_DIM), lambda b, h, s: (RET_HEADS + h, bidx(b, s), 0)),
            pl.BlockSpec((None, rows, RET_V_DIM), lambda b, h, s: (h, bidx(b, s), 0)),
            pl.BlockSpec((None, rows, RET_V_DIM), lambda b, h, s: (RET_HEADS + h, bidx(b, s), 0)),
            pl.BlockSpec((None, 1, RET_V_DIM), lambda b, h, s: (h, 0, 0)),
        ],
        out_specs=pl.BlockSpec((rows, RET_V_DIM), lambda b, h, s: (oidx(b, s), h)),
        out_shape=jax.ShapeDtypeStruct((t, RV_COLS), BF16),
        scratch_shapes=[
            pltpu.VMEM((RET_QK_DIM, RET_V_DIM), F32),
            pltpu.VMEM((RET_QK_DIM, RET_V_DIM), F32),
            pltpu.VMEM((nblk * nb, RET_QK_DIM, RET_V_DIM), BF16),
        ],
        compiler_params=_params(("parallel", "parallel", "arbitrary")),
        name="retention",
    )(dec_f.reshape(RET_HEADS, 1, 1), dec_b.reshape(RET_HEADS, 1, 1),
      rqk, rqk, rvg, rvg, gn_w.reshape(RET_HEADS, 1, RET_V_DIM))


def _diff_kernel(lq1_ref, lk1_ref, lq2_ref, lk2_ref, sub_ref, q_ref, k_ref, vt_ref, o_ref,
                 acc_scr, *, n_kv, tk, lambda_init):
    tq = q_ref.shape[1]
    acc_scr[...] = jnp.zeros_like(acc_scr)

    def body(j, carry):
        start = pl.multiple_of(j * tk, tk)
        vt = vt_ref[j]
        out = []
        for a in range(2):
            m_prev, l_prev = carry[a]
            st = lax.dot_general(k_ref[a, pl.ds(start, tk), :], q_ref[a], NT_DIMS,
                                 preferred_element_type=F32)
            m_new = jnp.maximum(m_prev, jnp.max(st, axis=0, keepdims=True))
            alpha = jnp.exp2(m_prev - m_new)
            p = jnp.exp2(st - m_new)
            l_new = alpha * l_prev + jnp.sum(p, axis=0, keepdims=True)
            acc_scr[a] = alpha * acc_scr[a] + jnp.dot(vt, p.astype(BF16),
                                                      preferred_element_type=F32)
            out.append((m_new, l_new))
        return tuple(out)

    init = (jnp.full((1, tq), -jnp.inf, F32), jnp.zeros((1, tq), F32))
    (_, l0), (_, l1) = lax.fori_loop(0, n_kv, body, (init, init))

    lam = (jnp.exp(jnp.sum(lq1_ref[...] * lk1_ref[...], axis=1, keepdims=True))
           - jnp.exp(jnp.sum(lq2_ref[...] * lk2_ref[...], axis=1, keepdims=True)) + lambda_init)
    ot = acc_scr[0] * (1.0 / l0) - lam * (acc_scr[1] * (1.0 / l1))
    o = ot.T
    o = o * lax.rsqrt(jnp.mean(jnp.square(o), axis=-1, keepdims=True) + RMS_EPS)
    o_ref[...] = (o * sub_ref[...] * (1.0 - lambda_init)).astype(o_ref.dtype)


def _diff_attention(dqk, dvt, lq1, lk1, lq2, lk2, subln_w, batch, seq, lambda_init):
    t = dqk.shape[1]
    tk = dvt.shape[3]
    n_kv = seq // tk
    tq = _tile(seq, ATTN_Q_ROWS)
    nq = seq // tq
    vec = lambda a: a.reshape(1, -1).astype(F32)
    vspec = lambda n: pl.BlockSpec((1, n), lambda b, h, i: (0, 0))
    return pl.pallas_call(
        functools.partial(_diff_kernel, n_kv=n_kv, tk=tk, lambda_init=lambda_init),
        grid=(batch, DIFF_HEADS, nq),
        in_specs=[
            vspec(DIFF_QK_DIM), vspec(DIFF_QK_DIM), vspec(DIFF_QK_DIM), vspec(DIFF_QK_DIM),
            vspec(DIFF_V_DIM),
            pl.BlockSpec((2, tq, DIFF_QK_DIM), lambda b, h, i: (h, b * nq + i, 0)),
            pl.BlockSpec((2, seq, DIFF_QK_DIM), lambda b, h, i: (DIFF_HEADS + h, b, 0)),
            pl.BlockSpec((None, n_kv, DIFF_V_DIM, tk), lambda b, h, i: (h, b, 0, 0)),
        ],
        out_specs=pl.BlockSpec((tq, DIFF_V_DIM), lambda b, h, i: (b * nq + i, h)),
        out_shape=jax.ShapeDtypeStruct((t, DV_COLS), BF16),
        scratch_shapes=[pltpu.VMEM((2, DIFF_V_DIM, tq), F32)],
        compiler_params=_params(("parallel", "parallel", "arbitrary")),
        name="diff_attention",
    )(vec(lq1), vec(lk1), vec(lq2), vec(lk2), vec(subln_w), dqk, dqk, dvt)


def _outproj_kernel(ro_ref, do_ref, wt_ref, wb_ref, x_ref, g_ref, b_ref, o32_ref, o16_ref,
                    *, nj, tn):
    j = pl.program_id(1)
    val = (DEEPNORM_ALPHA * x_ref[...]
           + jnp.dot(ro_ref[...], wt_ref[...], preferred_element_type=F32)
           + jnp.dot(do_ref[...], wb_ref[...], preferred_element_type=F32))
    for c in range(nj):
        @pl.when(j == c)
        def _(c=c):
            o32_ref[:, c * tn:(c + 1) * tn] = val

    @pl.when(j == nj - 1)
    def _():
        _layer_norm_rows(o32_ref, g_ref, b_ref, (o32_ref, o16_ref))


def _out_projection(ro, do, w_out, x, ln_g, ln_b):
    t = x.shape[0]
    tm = _tile(t, 512)
    tn = 512
    nj = D_MODEL // tn
    half = RV_COLS
    row = lambda i, j: (i, 0)
    return pl.pallas_call(
        functools.partial(_outproj_kernel, nj=nj, tn=tn),
        grid=(t // tm, nj),
        in_specs=[
            pl.BlockSpec((tm, half), row),
            pl.BlockSpec((tm, half), row),
            pl.BlockSpec((half, tn), lambda i, j: (0, j)),
            pl.BlockSpec((half, tn), lambda i, j: (1, j)),
            pl.BlockSpec((tm, tn), lambda i, j: (i, j)),
            pl.BlockSpec((1, D_MODEL), lambda i, j: (0, 0)),
            pl.BlockSpec((1, D_MODEL), lambda i, j: (0, 0)),
        ],
        out_specs=[pl.BlockSpec((tm, D_MODEL), row), pl.BlockSpec((tm, D_MODEL), row)],
        out_shape=[jax.ShapeDtypeStruct((t, D_MODEL), F32),
                   jax.ShapeDtypeStruct((t, D_MODEL), BF16)],
        compiler_params=_params(("parallel", "arbitrary")),
        name="out_projection",
    )(ro, do, w_out, w_out, x, ln_g.reshape(1, -1), ln_b.reshape(1, -1))


def _gate_up_kernel(x_ref, wg_ref, wu_ref, o_ref):
    x = x_ref[...]
    g = jnp.dot(x, wg_ref[...], preferred_element_type=F32)
    u = jnp.dot(x, wu_ref[...], preferred_element_type=F32)
    o_ref[...] = (_swish(g) * u).astype(o_ref.dtype)


def _gate_up(xb, wg, wu):
    t, d = xb.shape
    n = wg.shape[1]
    tm = _tile(t, 1024)
    tn = 512
    return pl.pallas_call(
        _gate_up_kernel,
        grid=(t // tm, n // tn),
        in_specs=[pl.BlockSpec((tm, d), lambda i, j: (i, 0)),
                  pl.BlockSpec((d, tn), lambda i, j: (0, j)),
                  pl.BlockSpec((d, tn), lambda i, j: (0, j))],
        out_specs=pl.BlockSpec((tm, tn), lambda i, j: (i, j)),
        out_shape=jax.ShapeDtypeStruct((t, n), BF16),
        compiler_params=_params(("parallel", "arbitrary")),
        name="ffn_gate_up",
    )(xb, wg, wu)


def _down_kernel(h_ref, w_ref, x_ref, g_ref, b_ref, o_ref, *, nk):
    k = pl.program_id(1)

    @pl.when(k == 0)
    def _():
        o_ref[...] = DEEPNORM_ALPHA * x_ref[...]

    h = h_ref[...]
    for c in range(o_ref.shape[1] // DOWN_COL_CHUNK):
        sl = slice(c * DOWN_COL_CHUNK, (c + 1) * DOWN_COL_CHUNK)
        o_ref[:, sl] += jnp.dot(h, w_ref[:, sl], preferred_element_type=F32)

    @pl.when(k == nk - 1)
    def _():
        _layer_norm_rows(o_ref, g_ref, b_ref, (o_ref,))


def _down_projection(h, wd, x1, ln_g, ln_b):
    t, kdim = h.shape
    tm = _tile(t, 512)
    tk = 1024
    nk = kdim // tk
    return pl.pallas_call(
        functools.partial(_down_kernel, nk=nk),
        grid=(t // tm, nk),
        in_specs=[pl.BlockSpec((tm, tk), lambda i, k: (i, k)),
                  pl.BlockSpec((tk, D_MODEL), lambda i, k: (k, 0)),
                  pl.BlockSpec((tm, D_MODEL), lambda i, k: (i, 0)),
                  pl.BlockSpec((1, D_MODEL), lambda i, k: (0, 0)),
                  pl.BlockSpec((1, D_MODEL), lambda i, k: (0, 0))],
        out_specs=pl.BlockSpec((tm, D_MODEL), lambda i, k: (i, 0)),
        out_shape=jax.ShapeDtypeStruct((t, D_MODEL), F32),
        compiler_params=_params(("parallel", "arbitrary")),
        name="ffn_down",
    )(h, wd, x1, ln_g.reshape(1, -1), ln_b.reshape(1, -1))


def _rope_tables(seq):
    pos = jnp.arange(seq, dtype=F32)

    def cos_sin(rot_dim, base):
        inv_freq = base ** (-jnp.arange(0, rot_dim, 2, dtype=F32) / rot_dim)
        ang = pos[:, None] * inv_freq[None, :]
        return jnp.cos(ang), jnp.sin(ang)

    c, s = cos_sin(RET_QK_DIM, RET_ROT_BASE)
    ret = jnp.concatenate([c, c, -s, s], axis=-1)
    ret_tab = jnp.stack([ret, ret * (RET_QK_DIM ** -0.5)])

    c, s = cos_sin(DIFF_ROT_DIM, ROPE_THETA)
    half = DIFF_ROT_DIM // 2
    rest = DIFF_QK_DIM - DIFF_ROT_DIM
    ones = jnp.ones((seq, rest), F32)
    zeros = jnp.zeros((seq, rest), F32)
    zh = jnp.zeros((seq, half), F32)
    a = jnp.concatenate([c, c, ones], axis=-1)
    b = jnp.concatenate([zh, s, zeros], axis=-1)
    c2 = jnp.concatenate([-s, zh, zeros], axis=-1)
    diff = jnp.concatenate([a, b, c2], axis=-1)
    diff_tab = jnp.stack([diff * (DIFF_QK_DIM ** -0.5 * LOG2_E), diff])
    return ret_tab, diff_tab


def _prepare_weights(w_in, w_out, w_gate, w_up, w_down):
    wb = w_in.astype(BF16)
    c0 = 2 * RQ_COLS
    c1 = c0 + 2 * RV_COLS
    c2 = c1 + 2 * DQ_COLS
    pad = D_FF_PAD - D_FF
    wg = jnp.pad(w_gate.astype(BF16), ((0, 0), (0, pad)))
    wu = jnp.pad(w_up.astype(BF16), ((0, 0), (0, pad)))
    wd = jnp.pad(w_down.astype(BF16), ((0, pad), (0, 0)))
    return wb[:, :c0], wb[:, c0:c1], wb[:, c1:c2], wb[:, c2:], w_out.astype(BF16), wg, wu, wd


def _encoder_layer(x, weights, dec_f, dec_b, gn_w, lq1, lk1, lq2, lk2, subln_w,
                   ln1_g, ln1_b, ln2_g, ln2_b, lambda_init):
    batch, seq, d = x.shape
    w_rqk, w_rvg, w_dqk, w_dv, w_out, wg, wu, wd = weights
    x2 = x.reshape(batch * seq, d)
    xb = x2.astype(BF16)
    ret_tab, diff_tab = _rope_tables(seq)

    rqk = _project(xb, w_rqk, seq, "rope_full", RET_QK_DIM, ret_tab, RQ_COLS)
    rvg = _project(xb, w_rvg, seq, "plain", RET_V_DIM)
    dqk = _project(xb, w_dqk, seq, "rope_partial", DIFF_QK_DIM, diff_tab, DQ_COLS)
    dvt = _project(xb, w_dv, seq, "plain_t", DIFF_V_DIM)

    ro = _retention(rqk, rvg, dec_f, dec_b, gn_w, batch, seq)
    do = _diff_attention(dqk, dvt, lq1, lk1, lq2, lk2, subln_w, batch, seq, lambda_init)

    x1, x1b = _out_projection(ro, do, w_out, x2, ln1_g, ln1_b)
    h = _gate_up(x1b, wg, wu)
    y = _down_projection(h, wd, x1, ln2_g, ln2_b)
    return y.reshape(batch, seq, d)


def kernel(x_prompt, x_sample, w_in, ret_decay_f, ret_decay_b, ret_gn_w, diff_lambda_q1,
           diff_lambda_k1, diff_lambda_q2, diff_lambda_k2, diff_subln_w, w_out, ln1_g, ln1_b,
           w_gate, w_up, w_down, ln2_g, ln2_b):
    y_prompt, y_sample = x_prompt, x_sample
    for l in range(DEPTH):
        lambda_init = 0.8 - 0.6 * math.exp(-0.3 * l)
        weights = _prepare_weights(w_in[l], w_out[l], w_gate[l], w_up[l], w_down[l])
        rest = (ret_decay_f[l], ret_decay_b[l], ret_gn_w[l], diff_lambda_q1[l], diff_lambda_k1[l],
                diff_lambda_q2[l], diff_lambda_k2[l], diff_subln_w[l], ln1_g[l], ln1_b[l],
                ln2_g[l], ln2_b[l], lambda_init)
        y_prompt = _encoder_layer(y_prompt, weights, *rest)
        y_sample = _encoder_layer(y_sample, weights, *rest)
    return (y_prompt, y_sample)
```

```python
import functools
import math

import jax
import jax.numpy as jnp
from jax import lax
from jax.experimental import pallas as pl
from jax.experimental.pallas import tpu as pltpu

D_MODEL = 4096
DEPTH = 1
RET_HEADS = 8
RET_QK_DIM = 128
RET_V_DIM = 256
RET_ROT_BASE = 10000.0
DIFF_HEADS = 8
DIFF_QK_DIM = 128
DIFF_V_DIM = 256
DIFF_ROT_DIM = DIFF_QK_DIM // 4
ROPE_THETA = 500000.0
D_FF = 11008
DEEPNORM_ALPHA = (2.0 * DEPTH) ** 0.25
LN_EPS = 1e-5
GN_EPS = 1e-6
RMS_EPS = 1e-5

RQ_COLS = RET_HEADS * RET_QK_DIM
RV_COLS = RET_HEADS * RET_V_DIM
DQ_COLS = DIFF_HEADS * 2 * DIFF_QK_DIM
DV_COLS = DIFF_HEADS * DIFF_V_DIM

LANES = 128
VMEM_LIMIT_BYTES = 56 * 1024 * 1024
D_FF_PAD = 11264
PROJ_ROWS = 1024
RET_CHUNK = 256
RET_BLOCK = 1024
ATTN_Q_ROWS = 512
DOWN_COL_CHUNK = 1024
LN_ROW_CHUNK = 64
LOG2_E = math.log2(math.e)

BF16 = jnp.bfloat16
F32 = jnp.float32

NT_DIMS = (((1,), (1,)), ((), ()))
TN_DIMS = (((0,), (0,)), ((), ()))


def _tile(n, pref):
    t = pref
    while t > 8 and n % t:
        t //= 2
    assert n % t == 0, (n, pref)
    return t


def _params(sem):
    return pltpu.CompilerParams(dimension_semantics=sem, vmem_limit_bytes=VMEM_LIMIT_BYTES)


def _swish(g):
    return g * (1.0 / (1.0 + jnp.exp(-g)))


def _layer_norm_rows(src_ref, g_ref, b_ref, outs):
    def rows(r, carry):
        rs = pl.ds(pl.multiple_of(r * LN_ROW_CHUNK, LN_ROW_CHUNK), LN_ROW_CHUNK)
        pre = src_ref[rs, :]
        mu = jnp.mean(pre, axis=-1, keepdims=True)
        var = jnp.mean(jnp.square(pre - mu), axis=-1, keepdims=True)
        y = (pre - mu) * lax.rsqrt(var + LN_EPS) * g_ref[...] + b_ref[...]
        for o_ref in outs:
            o_ref[rs, :] = y.astype(o_ref.dtype)
        return carry

    lax.fori_loop(0, src_ref.shape[0] // LN_ROW_CHUNK, rows, 0)


def _proj_kernel(x_ref, w_ref, *rest, mode, hw):
    if mode in ("plain", "plain_t"):
        (o_ref,) = rest
        tab_ref = None
    else:
        tab_ref, o_ref = rest
    tm = x_ref.shape[0]
    n_split = 1 if mode == "plain" else 2
    rows = tm // n_split
    for r in range(n_split):
        rs = slice(r * rows, (r + 1) * rows)
        acc = jnp.dot(x_ref[rs, :], w_ref[...], preferred_element_type=F32)
        for c in range(o_ref.shape[0]):
            blk = acc[:, c * hw:(c + 1) * hw]
            if mode == "rope_full":
                cos = tab_ref[rs, 0:LANES]
                sin = tab_ref[rs, LANES:2 * LANES]
                blk = blk * cos + pltpu.roll(blk, LANES // 2, 1) * sin
            elif mode == "rope_partial":
                half = DIFF_ROT_DIM // 2
                a = tab_ref[rs, 0:LANES]
                b = tab_ref[rs, LANES:2 * LANES]
                c2 = tab_ref[rs, 2 * LANES:3 * LANES]
                blk = (blk * a + pltpu.roll(blk, half, 1) * b
                       + pltpu.roll(blk, LANES - half, 1) * c2)
            if mode == "plain_t":
                o_ref[c, :, rs] = blk.T.astype(o_ref.dtype)
            else:
                o_ref[c, rs, :] = blk.astype(o_ref.dtype)


def _project(xb, w, seq, mode, hw, tab=None, tab_split=None):
    t, d = xb.shape
    n = w.shape[1]
    tm = _tile(seq, PROJ_ROWS)
    tn = _tile(n, 512)
    grid = (t // tm, n // tn)
    in_specs = [pl.BlockSpec((tm, d), lambda i, j: (i, 0)),
                pl.BlockSpec((d, tn), lambda i, j: (0, j))]
    args = [xb, w]
    if tab is not None:
        per_seq = seq // tm
        split = tab_split // tn
        in_specs.append(pl.BlockSpec((tm, tab.shape[1] // 2),
                                     lambda i, j: (i % per_seq, j // split)))
        args.append(tab)
    if mode == "plain_t":
        out_spec = pl.BlockSpec((tn // hw, None, hw, tm), lambda i, j: (j, i, 0, 0))
        out_shape = jax.ShapeDtypeStruct((n // hw, t // tm, hw, tm), BF16)
    else:
        out_spec = pl.BlockSpec((tn // hw, tm, hw), lambda i, j: (j, i, 0))
        out_shape = jax.ShapeDtypeStruct((n // hw, t, hw), BF16)
    return pl.pallas_call(
        functools.partial(_proj_kernel, mode=mode, hw=hw),
        grid=grid,
        in_specs=in_specs,
        out_specs=out_spec,
        out_shape=out_shape,
        compiler_params=_params(("parallel", "arbitrary")),
        name="proj_" + mode,
    )(*args)


def _ret_kernel(decf_ref, decb_ref, q_ref, k_ref, v_ref, g_ref, gnw_ref, o_ref,
                sf_scr, sb_scr, sball_scr, *, nblk, nb, chunk):
    s = pl.program_id(2)
    lg_f = -jnp.exp(decf_ref[...])
    lg_b = -jnp.exp(decb_ref[...])
    row = lax.broadcasted_iota(jnp.int32, (chunk, 1), 0).astype(F32)

    @pl.when(s == 0)
    def _():
        sb_scr[...] = jnp.zeros_like(sb_scr)

    @pl.when(s < nblk)
    def _():
        blk = nblk - 1 - s
        zeta_b = jnp.exp(row * lg_b)
        decay_b = jnp.exp(chunk * lg_b)
        state = sb_scr[...]
        for ci in reversed(range(nb)):
            rs = slice(ci * chunk, (ci + 1) * chunk)
            sball_scr[blk * nb + ci] = state.astype(BF16)
            kz = (k_ref[rs, :].astype(F32) * zeta_b).astype(BF16)
            state = decay_b * state + lax.dot_general(kz, v_ref[rs, :], TN_DIMS,
                                                      preferred_element_type=F32)
        sb_scr[...] = state

    @pl.when(s == nblk)
    def _():
        sf_scr[...] = jnp.zeros_like(sf_scr)

    @pl.when(s >= nblk)
    def _():
        blk = s - nblk
        rel = (lax.broadcasted_iota(jnp.int32, (chunk, chunk), 0)
               - lax.broadcasted_iota(jnp.int32, (chunk, chunk), 1)).astype(F32)
        dmat = (jnp.where(rel >= 0, jnp.exp(jnp.maximum(rel, 0.0) * lg_f), 0.0)
                + jnp.where(rel <= 0, jnp.exp(jnp.maximum(-rel, 0.0) * lg_b), 0.0))
        xi_f = jnp.exp((row + 1.0) * lg_f)
        xi_b = jnp.exp((chunk - row) * lg_b)
        zeta_f = jnp.exp((chunk - 1.0 - row) * lg_f)
        decay_f = jnp.exp(chunk * lg_f)
        gnw = gnw_ref[...]
        state = sf_scr[...]
        for ci in range(nb):
            rs = slice(ci * chunk, (ci + 1) * chunk)
            q = q_ref[rs, :]
            k = k_ref[rs, :]
            v = v_ref[rs, :]
            sc = lax.dot_general(q, k, NT_DIMS, preferred_element_type=F32)
            qf = q.astype(F32)
            o = jnp.dot((sc * dmat).astype(BF16), v, preferred_element_type=F32)
            o += jnp.dot((qf * xi_f).astype(BF16), state.astype(BF16), preferred_element_type=F32)
            o += jnp.dot((qf * xi_b).astype(BF16), sball_scr[blk * nb + ci],
                         preferred_element_type=F32)
            kz = (k.astype(F32) * zeta_f).astype(BF16)
            state = decay_f * state + lax.dot_general(kz, v, TN_DIMS, preferred_element_type=F32)
            mu = jnp.mean(o, axis=-1, keepdims=True)
            var = jnp.mean(jnp.square(o - mu), axis=-1, keepdims=True)
            on = (o - mu) * lax.rsqrt(var + GN_EPS) * gnw
            o_ref[rs, :] = (_swish(g_ref[rs, :].astype(F32)) * on).astype(o_ref.dtype)
        sf_scr[...] = state


def _retention(rqk, rvg, dec_f, dec_b, gn_w, batch, seq):
    t = rqk.shape[1]
    rows = _tile(seq, RET_BLOCK)
    chunk = _tile(rows, RET_CHUNK)
    nb = rows // chunk
    nblk = seq // rows

    def bidx(b, s):
        return b * nblk + jnp.where(s < nblk, nblk - 1 - s, s - nblk)

    def oidx(b, s):
        return b * nblk + jnp.maximum(s - nblk, 0)

    return pl.pallas_call(
        functools.partial(_ret_kernel, nblk=nblk, nb=nb, chunk=chunk),
        grid=(batch, RET_HEADS, 2 * nblk),
        in_specs=[
            pl.BlockSpec((None, 1, 1), lambda b, h, s: (h, 0, 0)),
            pl.BlockSpec((None, 1, 1), lambda b, h, s: (h, 0, 0)),
            pl.BlockSpec((None, rows, RET_QK_DIM), lambda b, h, s: (h, oidx(b, s), 0)),
            pl.BlockSpec((None, rows, RET_QK_DIM), lambda b, h, s: (RET_HEADS + h, bidx(b, s), 0)),
            pl.BlockSpec((None, rows, RET_V_DIM), lambda b, h, s: (h, bidx(b, s), 0)),
            pl.BlockSpec((None, rows, RET_V_DIM), lambda b, h, s: (RET_HEADS + h, oidx(b, s), 0)),
            pl.BlockSpec((None, 1, RET_V_DIM), lambda b, h, s: (h, 0, 0)),
        ],
        out_specs=pl.BlockSpec((rows, RET_V_DIM), lambda b, h, s: (oidx(b, s), h)),
        out_shape=jax.ShapeDtypeStruct((t, RV_COLS), BF16),
        scratch_shapes=[
            pltpu.VMEM((RET_QK_DIM, RET_V_DIM), F32),
            pltpu.VMEM((RET_QK_DIM, RET_V_DIM), F32),
            pltpu.VMEM((nblk * nb, RET_QK_DIM, RET_V_DIM), BF16),
        ],
        compiler_params=_params(("parallel", "parallel", "arbitrary")),
        name="retention",
    )(dec_f.reshape(RET_HEADS, 1, 1), dec_b.reshape(RET_HEADS, 1, 1),
      rqk, rqk, rvg, rvg, gn_w.reshape(RET_HEADS, 1, RET_V_DIM))


def _diff_kernel(lq1_ref, lk1_ref, lq2_ref, lk2_ref, sub_ref, q_ref, k_ref, vt_ref, o_ref,
                 acc_scr, s0_scr, s1_scr, *, n_kv, tk, lambda_init):
    tq = q_ref.shape[1]
    acc_scr[...] = jnp.zeros_like(acc_scr)

    def scores(j, s_scr):
        start = pl.multiple_of(j * tk, tk)
        for a in range(2):
            s_scr[a] = lax.dot_general(k_ref[a, pl.ds(start, tk), :], q_ref[a], NT_DIMS,
                                       preferred_element_type=F32)

    def absorb(j, s_scr, carry):
        vt = vt_ref[j]
        out = []
        for a in range(2):
            m_prev, l_prev = carry[a]
            st = s_scr[a]
            m_new = jnp.maximum(m_prev, jnp.max(st, axis=0, keepdims=True))
            alpha = jnp.exp2(m_prev - m_new)
            p = jnp.exp2(st - m_new)
            l_new = alpha * l_prev + jnp.sum(p, axis=0, keepdims=True)
            acc_scr[a] = alpha * acc_scr[a] + jnp.dot(vt, p.astype(BF16),
                                                      preferred_element_type=F32)
            out.append((m_new, l_new))
        return tuple(out)

    init = (jnp.full((1, tq), -jnp.inf, F32), jnp.zeros((1, tq), F32))
    carry = (init, init)
    scores(0, s0_scr)
    if n_kv > 1:
        def pair(i, carry):
            scores(2 * i + 1, s1_scr)
            carry = absorb(2 * i, s0_scr, carry)
            scores(2 * i + 2, s0_scr)
            return absorb(2 * i + 1, s1_scr, carry)

        carry = lax.fori_loop(0, n_kv // 2 - 1, pair, carry)
        scores(n_kv - 1, s1_scr)
        carry = absorb(n_kv - 2, s0_scr, carry)
        carry = absorb(n_kv - 1, s1_scr, carry)
    else:
        carry = absorb(0, s0_scr, carry)
    (_, l0), (_, l1) = carry

    lam = (jnp.exp(jnp.sum(lq1_ref[...] * lk1_ref[...], axis=1, keepdims=True))
           - jnp.exp(jnp.sum(lq2_ref[...] * lk2_ref[...], axis=1, keepdims=True)) + lambda_init)
    ot = acc_scr[0] * (1.0 / l0) - lam * (acc_scr[1] * (1.0 / l1))
    o = ot.T
    o = o * lax.rsqrt(jnp.mean(jnp.square(o), axis=-1, keepdims=True) + RMS_EPS)
    o_ref[...] = (o * sub_ref[...] * (1.0 - lambda_init)).astype(o_ref.dtype)


def _diff_attention(dqk, dvt, lq1, lk1, lq2, lk2, subln_w, batch, seq, lambda_init):
    t = dqk.shape[1]
    tk = dvt.shape[3]
    n_kv = seq // tk
    assert n_kv == 1 or n_kv % 2 == 0, n_kv
    tq = _tile(seq, ATTN_Q_ROWS)
    nq = seq // tq
    vec = lambda a: a.reshape(1, -1).astype(F32)
    vspec = lambda n: pl.BlockSpec((1, n), lambda b, h, i: (0, 0))
    return pl.pallas_call(
        functools.partial(_diff_kernel, n_kv=n_kv, tk=tk, lambda_init=lambda_init),
        grid=(batch, DIFF_HEADS, nq),
        in_specs=[
            vspec(DIFF_QK_DIM), vspec(DIFF_QK_DIM), vspec(DIFF_QK_DIM), vspec(DIFF_QK_DIM),
            vspec(DIFF_V_DIM),
            pl.BlockSpec((2, tq, DIFF_QK_DIM), lambda b, h, i: (h, b * nq + i, 0)),
            pl.BlockSpec((2, seq, DIFF_QK_DIM), lambda b, h, i: (DIFF_HEADS + h, b, 0)),
            pl.BlockSpec((None, n_kv, DIFF_V_DIM, tk), lambda b, h, i: (h, b, 0, 0)),
        ],
        out_specs=pl.BlockSpec((tq, DIFF_V_DIM), lambda b, h, i: (b * nq + i, h)),
        out_shape=jax.ShapeDtypeStruct((t, DV_COLS), BF16),
        scratch_shapes=[pltpu.VMEM((2, DIFF_V_DIM, tq), F32),
                        pltpu.VMEM((2, tk, tq), F32),
                        pltpu.VMEM((2, tk, tq), F32)],
        compiler_params=_params(("parallel", "parallel", "arbitrary")),
        name="diff_attention",
    )(vec(lq1), vec(lk1), vec(lq2), vec(lk2), vec(subln_w), dqk, dqk, dvt)


def _outproj_kernel(ro_ref, do_ref, w_ref, x_ref, g_ref, b_ref, o32_ref, o16_ref, mix_scr,
                    *, nj, tn):
    j = pl.program_id(1)
    half = ro_ref.shape[1]

    @pl.when(j == 0)
    def _():
        mix_scr[:, :half] = ro_ref[...]
        mix_scr[:, half:] = do_ref[...]

    val = DEEPNORM_ALPHA * x_ref[...] + jnp.dot(mix_scr[...], w_ref[...],
                                                preferred_element_type=F32)
    for c in range(nj):
        @pl.when(j == c)
        def _(c=c):
            o32_ref[:, c * tn:(c + 1) * tn] = val

    @pl.when(j == nj - 1)
    def _():
        _layer_norm_rows(o32_ref, g_ref, b_ref, (o32_ref, o16_ref))


def _out_projection(ro, do, w_out, x, ln_g, ln_b):
    t = x.shape[0]
    tm = _tile(t, 512)
    tn = 512
    nj = D_MODEL // tn
    half = RV_COLS
    row = lambda i, j: (i, 0)
    return pl.pallas_call(
        functools.partial(_outproj_kernel, nj=nj, tn=tn),
        grid=(t // tm, nj),
        in_specs=[
            pl.BlockSpec((tm, half), row),
            pl.BlockSpec((tm, half), row),
            pl.BlockSpec((2 * half, tn), lambda i, j: (0, j)),
            pl.BlockSpec((tm, tn), lambda i, j: (i, j)),
            pl.BlockSpec((1, D_MODEL), lambda i, j: (0, 0)),
            pl.BlockSpec((1, D_MODEL), lambda i, j: (0, 0)),
        ],
        out_specs=[pl.BlockSpec((tm, D_MODEL), row), pl.BlockSpec((tm, D_MODEL), row)],
        out_shape=[jax.ShapeDtypeStruct((t, D_MODEL), F32),
                   jax.ShapeDtypeStruct((t, D_MODEL), BF16)],
        scratch_shapes=[pltpu.VMEM((tm, 2 * half), BF16)],
        compiler_params=_params(("parallel", "arbitrary")),
        name="out_projection",
    )(ro, do, w_out, x, ln_g.reshape(1, -1), ln_b.reshape(1, -1))


def _gate_up_kernel(x_ref, wg_ref, wu_ref, o_ref):
    x = x_ref[...]
    g = jnp.dot(x, wg_ref[...], preferred_element_type=F32)
    u = jnp.dot(x, wu_ref[...], preferred_element_type=F32)
    o_ref[...] = (_swish(g) * u).astype(o_ref.dtype)


def _gate_up(xb, wg, wu):
    t, d = xb.shape
    n = wg.shape[1]
    tm = _tile(t, 1024)
    tn = 512
    return pl.pallas_call(
        _gate_up_kernel,
        grid=(t // tm, n // tn),
        in_specs=[pl.BlockSpec((tm, d), lambda i, j: (i, 0)),
                  pl.BlockSpec((d, tn), lambda i, j: (0, j)),
                  pl.BlockSpec((d, tn), lambda i, j: (0, j))],
        out_specs=pl.BlockSpec((tm, tn), lambda i, j: (i, j)),
        out_shape=jax.ShapeDtypeStruct((t, n), BF16),
        compiler_params=_params(("parallel", "arbitrary")),
        name="ffn_gate_up",
    )(xb, wg, wu)


def _down_kernel(h_ref, w_ref, x_ref, g_ref, b_ref, o_ref, *, nk):
    k = pl.program_id(1)

    @pl.when(k == 0)
    def _():
        o_ref[...] = DEEPNORM_ALPHA * x_ref[...]

    h = h_ref[...]
    for c in range(o_ref.shape[1] // DOWN_COL_CHUNK):
        sl = slice(c * DOWN_COL_CHUNK, (c + 1) * DOWN_COL_CHUNK)
        o_ref[:, sl] += jnp.dot(h, w_ref[:, sl], preferred_element_type=F32)

    @pl.when(k == nk - 1)
    def _():
        _layer_norm_rows(o_ref, g_ref, b_ref, (o_ref,))


def _down_projection(h, wd, x1, ln_g, ln_b):
    t, kdim = h.shape
    tm = _tile(t, 512)
    tk = 1024
    nk = kdim // tk
    return pl.pallas_call(
        functools.partial(_down_kernel, nk=nk),
        grid=(t // tm, nk),
        in_specs=[pl.BlockSpec((tm, tk), lambda i, k: (i, k)),
                  pl.BlockSpec((tk, D_MODEL), lambda i, k: (k, 0)),
                  pl.BlockSpec((tm, D_MODEL), lambda i, k: (i, 0)),
                  pl.BlockSpec((1, D_MODEL), lambda i, k: (0, 0)),
                  pl.BlockSpec((1, D_MODEL), lambda i, k: (0, 0))],
        out_specs=pl.BlockSpec((tm, D_MODEL), lambda i, k: (i, 0)),
        out_shape=jax.ShapeDtypeStruct((t, D_MODEL), F32),
        compiler_params=_params(("parallel", "arbitrary")),
        name="ffn_down",
    )(h, wd, x1, ln_g.reshape(1, -1), ln_b.reshape(1, -1))


def _rope_tables(seq):
    pos = jnp.arange(seq, dtype=F32)

    def cos_sin(rot_dim, base):
        inv_freq = base ** (-jnp.arange(0, rot_dim, 2, dtype=F32) / rot_dim)
        ang = pos[:, None] * inv_freq[None, :]
        return jnp.cos(ang), jnp.sin(ang)

    c, s = cos_sin(RET_QK_DIM, RET_ROT_BASE)
    ks = RET_QK_DIM ** -0.5
    ret_tab = jnp.concatenate([c, c, -s, s, c * ks, c * ks, -s * ks, s * ks], axis=-1)

    c, s = cos_sin(DIFF_ROT_DIM, ROPE_THETA)
    half = DIFF_ROT_DIM // 2
    rest = DIFF_QK_DIM - DIFF_ROT_DIM
    ones = jnp.ones((seq, rest), F32)
    zeros = jnp.zeros((seq, rest), F32)
    zh = jnp.zeros((seq, half), F32)
    qs = DIFF_QK_DIM ** -0.5 * LOG2_E
    diff_tab = jnp.concatenate(
        [c * qs, c * qs, ones * qs, zh, s * qs, zeros, -s * qs, zh, zeros,
         c, c, ones, zh, s, zeros, -s, zh, zeros], axis=-1)
    return ret_tab, diff_tab


def _prepare_weights(w_in, w_out, w_gate, w_up, w_down):
    wb = w_in.astype(BF16)
    c0 = 2 * RQ_COLS
    c1 = c0 + 2 * RV_COLS
    c2 = c1 + 2 * DQ_COLS
    pad = D_FF_PAD - D_FF
    zc = jnp.zeros((w_gate.shape[0], pad), BF16)
    wg = jnp.concatenate([w_gate.astype(BF16), zc], axis=1)
    wu = jnp.concatenate([w_up.astype(BF16), zc], axis=1)
    wd = jnp.concatenate([w_down.astype(BF16), jnp.zeros((pad, w_down.shape[1]), BF16)], axis=0)
    return wb[:, :c0], wb[:, c0:c1], wb[:, c1:c2], wb[:, c2:], w_out.astype(BF16), wg, wu, wd


def _encoder_layer(x, weights, tables, dec_f, dec_b, gn_w, lq1, lk1, lq2, lk2, subln_w,
                   ln1_g, ln1_b, ln2_g, ln2_b, lambda_init):
    batch, seq, d = x.shape
    w_rqk, w_rvg, w_dqk, w_dv, w_out, wg, wu, wd = weights
    x2 = x.reshape(batch * seq, d)
    xb = x2.astype(BF16)
    ret_tab, diff_tab = tables

    rqk = _project(xb, w_rqk, seq, "rope_full", RET_QK_DIM, ret_tab, RQ_COLS)
    rvg = _project(xb, w_rvg, seq, "plain", RET_V_DIM)
    dqk = _project(xb, w_dqk, seq, "rope_partial", DIFF_QK_DIM, diff_tab, DQ_COLS)
    dvt = _project(xb, w_dv, seq, "plain_t", DIFF_V_DIM)

    ro = _retention(rqk, rvg, dec_f, dec_b, gn_w, batch, seq)
    do = _diff_attention(dqk, dvt, lq1, lk1, lq2, lk2, subln_w, batch, seq, lambda_init)

    x1, x1b = _out_projection(ro, do, w_out, x2, ln1_g, ln1_b)
    h = _gate_up(x1b, wg, wu)
    y = _down_projection(h, wd, x1, ln2_g, ln2_b)
    return y.reshape(batch, seq, d)


def kernel(x_prompt, x_sample, w_in, ret_decay_f, ret_decay_b, ret_gn_w, diff_lambda_q1,
           diff_lambda_k1, diff_lambda_q2, diff_lambda_k2, diff_subln_w, w_out, ln1_g, ln1_b,
           w_gate, w_up, w_down, ln2_g, ln2_b):
    y_prompt, y_sample = x_prompt, x_sample
    for l in range(DEPTH):
        lambda_init = 0.8 - 0.6 * math.exp(-0.3 * l)
        weights = _prepare_weights(w_in[l], w_out[l], w_gate[l], w_up[l], w_down[l])
        tables = _rope_tables(max(y_prompt.shape[1], y_sample.shape[1]))
        rest = (tables, ret_decay_f[l], ret_decay_b[l], ret_gn_w[l], diff_lambda_q1[l], diff_lambda_k1[l],
                diff_lambda_q2[l], diff_lambda_k2[l], diff_subln_w[l], ln1_g[l], ln1_b[l],
                ln2_g[l], ln2_b[l], lambda_init)
        y_prompt = _encoder_layer(y_prompt, weights, *rest)
        y_sample = _encoder_layer(y_sample, weights, *rest)
    return (y_prompt, y_sample)
```

```python
import functools
import math

import jax
import jax.numpy as jnp
from jax import lax
from jax.experimental import pallas as pl
from jax.experimental.pallas import tpu as pltpu

D_MODEL = 4096
DEPTH = 1
RET_HEADS = 8
RET_QK_DIM = 128
RET_V_DIM = 256
RET_ROT_BASE = 10000.0
DIFF_HEADS = 8
DIFF_QK_DIM = 128
DIFF_V_DIM = 256
DIFF_ROT_DIM = DIFF_QK_DIM // 4
ROPE_THETA = 500000.0
D_FF = 11008
DEEPNORM_ALPHA = (2.0 * DEPTH) ** 0.25
LN_EPS = 1e-5
GN_EPS = 1e-6
RMS_EPS = 1e-5

RQ_COLS = RET_HEADS * RET_QK_DIM
RV_COLS = RET_HEADS * RET_V_DIM
DQ_COLS = DIFF_HEADS * 2 * DIFF_QK_DIM
DV_COLS = DIFF_HEADS * DIFF_V_DIM

LANES = 128
VMEM_LIMIT_BYTES = 56 * 1024 * 1024
D_FF_PAD = 11264
PROJ_ROWS = 1024
RET_CHUNK = 256
RET_BLOCK = 1024
ATTN_Q_ROWS = 512
ATTN_Q_BLOCKS = 4
DOWN_COL_CHUNK = 2048
LN_ROW_CHUNK = 64
LOG2_E = math.log2(math.e)

BF16 = jnp.bfloat16
F32 = jnp.float32

NT_DIMS = (((1,), (1,)), ((), ()))
TN_DIMS = (((0,), (0,)), ((), ()))


def _tile(n, pref):
    t = pref
    while t > 8 and n % t:
        t //= 2
    assert n % t == 0, (n, pref)
    return t


def _params(sem):
    return pltpu.CompilerParams(dimension_semantics=sem, vmem_limit_bytes=VMEM_LIMIT_BYTES)


def _swish(g):
    return g * (1.0 / (1.0 + jnp.exp(-g)))


def _layer_norm(pre, g, b):
    mu = jnp.mean(pre, axis=-1, keepdims=True)
    var = jnp.mean(jnp.square(pre - mu), axis=-1, keepdims=True)
    return (pre - mu) * lax.rsqrt(var + LN_EPS) * g + b


def _three_phase(i, n, first, middle, last):
    pl.when(i == 0)(first)
    pl.when(jnp.logical_and(i > 0, i < n))(middle)
    pl.when(i == n)(last)


def _proj_kernel(x_ref, w_ref, *rest, mode, hw):
    if mode in ("plain", "plain_t"):
        (o_ref,) = rest
        tab_ref = None
    else:
        tab_ref, o_ref = rest
    tm = x_ref.shape[0]
    n_split = 1 if mode == "plain" else 2
    rows = tm // n_split
    for r in range(n_split):
        rs = slice(r * rows, (r + 1) * rows)
        acc = jnp.dot(x_ref[rs, :], w_ref[...], preferred_element_type=F32)
        for c in range(o_ref.shape[0]):
            blk = acc[:, c * hw:(c + 1) * hw]
            if mode == "rope_full":
                cos = tab_ref[rs, 0:LANES]
                sin = tab_ref[rs, LANES:2 * LANES]
                blk = blk * cos + pltpu.roll(blk, LANES // 2, 1) * sin
            elif mode == "rope_partial":
                half = DIFF_ROT_DIM // 2
                a = tab_ref[rs, 0:LANES]
                b = tab_ref[rs, LANES:2 * LANES]
                c2 = tab_ref[rs, 2 * LANES:3 * LANES]
                blk = (blk * a + pltpu.roll(blk, half, 1) * b
                       + pltpu.roll(blk, LANES - half, 1) * c2)
            if mode == "plain_t":
                o_ref[c, :, rs] = blk.T.astype(o_ref.dtype)
            else:
                o_ref[c, rs, :] = blk.astype(o_ref.dtype)


def _project(xb, w, seq, mode, hw, tab=None, tab_split=None):
    t, d = xb.shape
    n = w.shape[1]
    tm = _tile(seq, PROJ_ROWS)
    tn = _tile(n, 512)
    grid = (t // tm, n // tn)
    in_specs = [pl.BlockSpec((tm, d), lambda i, j: (i, 0)),
                pl.BlockSpec((d, tn), lambda i, j: (0, j))]
    args = [xb, w]
    if tab is not None:
        per_seq = seq // tm
        split = tab_split // tn
        in_specs.append(pl.BlockSpec((tm, tab.shape[1] // 2),
                                     lambda i, j: (i % per_seq, j // split)))
        args.append(tab)
    if mode == "plain_t":
        out_spec = pl.BlockSpec((tn // hw, None, hw, tm), lambda i, j: (j, i, 0, 0))
        out_shape = jax.ShapeDtypeStruct((n // hw, t // tm, hw, tm), BF16)
    else:
        out_spec = pl.BlockSpec((tn // hw, tm, hw), lambda i, j: (j, i, 0))
        out_shape = jax.ShapeDtypeStruct((n // hw, t, hw), BF16)
    return pl.pallas_call(
        functools.partial(_proj_kernel, mode=mode, hw=hw),
        grid=grid,
        in_specs=in_specs,
        out_specs=out_spec,
        out_shape=out_shape,
        compiler_params=_params(("parallel", "arbitrary")),
        name="proj_" + mode,
    )(*args)


def _ret_kernel(decf_ref, decb_ref, q_ref, k_ref, v_ref, g_ref, gnw_ref, o_ref,
                sf_scr, sb_scr, sball_scr, *, nblk, nb, chunk):
    s = pl.program_id(2)
    lg_f = -jnp.exp(decf_ref[...])
    lg_b = -jnp.exp(decb_ref[...])
    row = lax.broadcasted_iota(jnp.int32, (chunk, 1), 0).astype(F32)

    @pl.when(s == 0)
    def _():
        sb_scr[...] = jnp.zeros_like(sb_scr)

    @pl.when(s < nblk)
    def _():
        blk = nblk - 1 - s
        zeta_b = jnp.exp(row * lg_b)
        decay_b = jnp.exp(chunk * lg_b)
        state = sb_scr[...]
        for ci in reversed(range(nb)):
            rs = slice(ci * chunk, (ci + 1) * chunk)
            sball_scr[blk * nb + ci] = state.astype(BF16)
            kz = (k_ref[rs, :].astype(F32) * zeta_b).astype(BF16)
            state = decay_b * state + lax.dot_general(kz, v_ref[rs, :], TN_DIMS,
                                                      preferred_element_type=F32)
        sb_scr[...] = state

    @pl.when(s == nblk)
    def _():
        sf_scr[...] = jnp.zeros_like(sf_scr)

    @pl.when(s >= nblk)
    def _():
        blk = s - nblk
        rel = (lax.broadcasted_iota(jnp.int32, (chunk, chunk), 0)
               - lax.broadcasted_iota(jnp.int32, (chunk, chunk), 1)).astype(F32)
        dmat = (jnp.where(rel >= 0, jnp.exp(jnp.maximum(rel, 0.0) * lg_f), 0.0)
                + jnp.where(rel <= 0, jnp.exp(jnp.maximum(-rel, 0.0) * lg_b), 0.0))
        xi_f = jnp.exp((row + 1.0) * lg_f)
        xi_b = jnp.exp((chunk - row) * lg_b)
        zeta_f = jnp.exp((chunk - 1.0 - row) * lg_f)
        decay_f = jnp.exp(chunk * lg_f)
        gnw = gnw_ref[...]
        state = sf_scr[...]
        for ci in range(nb):
            rs = slice(ci * chunk, (ci + 1) * chunk)
            q = q_ref[rs, :]
            k = k_ref[rs, :]
            v = v_ref[rs, :]
            sc = lax.dot_general(q, k, NT_DIMS, preferred_element_type=F32)
            qf = q.astype(F32)
            o = jnp.dot((sc * dmat).astype(BF16), v, preferred_element_type=F32)
            o += jnp.dot((qf * xi_f).astype(BF16), state.astype(BF16), preferred_element_type=F32)
            o += jnp.dot((qf * xi_b).astype(BF16), sball_scr[blk * nb + ci],
                         preferred_element_type=F32)
            kz = (k.astype(F32) * zeta_f).astype(BF16)
            state = decay_f * state + lax.dot_general(kz, v, TN_DIMS, preferred_element_type=F32)
            mu = jnp.mean(o, axis=-1, keepdims=True)
            var = jnp.mean(jnp.square(o - mu), axis=-1, keepdims=True)
            on = (o - mu) * lax.rsqrt(var + GN_EPS) * gnw
            o_ref[rs, :] = (_swish(g_ref[rs, :].astype(F32)) * on).astype(o_ref.dtype)
        sf_scr[...] = state


def _retention(rqk, rvg, dec_f, dec_b, gn_w, batch, seq):
    t = rqk.shape[1]
    rows = _tile(seq, RET_BLOCK)
    chunk = _tile(rows, RET_CHUNK)
    nb = rows // chunk
    nblk = seq // rows

    def bidx(b, s):
        return b * nblk + jnp.where(s < nblk, nblk - 1 - s, s - nblk)

    def oidx(b, s):
        return b * nblk + jnp.maximum(s - nblk, 0)

    return pl.pallas_call(
        functools.partial(_ret_kernel, nblk=nblk, nb=nb, chunk=chunk),
        grid=(batch, RET_HEADS, 2 * nblk),
        in_specs=[
            pl.BlockSpec((None, 1, 1), lambda b, h, s: (h, 0, 0)),
            pl.BlockSpec((None, 1, 1), lambda b, h, s: (h, 0, 0)),
            pl.BlockSpec((None, rows, RET_QK_DIM), lambda b, h, s: (h, oidx(b, s), 0)),
            pl.BlockSpec((None, rows, RET_QK_DIM), lambda b, h, s: (RET_HEADS + h, bidx(b, s), 0)),
            pl.BlockSpec((None, rows, RET_V_DIM), lambda b, h, s: (h, bidx(b, s), 0)),
            pl.BlockSpec((None, rows, RET_V_DIM), lambda b, h, s: (RET_HEADS + h, oidx(b, s), 0)),
            pl.BlockSpec((None, 1, RET_V_DIM), lambda b, h, s: (h, 0, 0)),
        ],
        out_specs=pl.BlockSpec((rows, RET_V_DIM), lambda b, h, s: (oidx(b, s), h)),
        out_shape=jax.ShapeDtypeStruct((t, RV_COLS), BF16),
        scratch_shapes=[
            pltpu.VMEM((RET_QK_DIM, RET_V_DIM), F32),
            pltpu.VMEM((RET_QK_DIM, RET_V_DIM), F32),
            pltpu.VMEM((nblk * nb, RET_QK_DIM, RET_V_DIM), BF16),
        ],
        compiler_params=_params(("parallel", "parallel", "arbitrary")),
        name="retention",
    )(dec_f.reshape(RET_HEADS, 1, 1), dec_b.reshape(RET_HEADS, 1, 1),
      rqk, rqk, rvg, rvg, gn_w.reshape(RET_HEADS, 1, RET_V_DIM))


def _diff_kernel(lq1_ref, lk1_ref, lq2_ref, lk2_ref, sub_ref, q_ref, k_ref, vt_ref, o_ref,
                 acc_scr, s0_scr, s1_scr, *, n_kv, tk, tq, lambda_init):
    n_blocks = q_ref.shape[1] // tq
    bufs = (s0_scr, s1_scr)
    lam = (jnp.exp(jnp.sum(lq1_ref[...] * lk1_ref[...], axis=1, keepdims=True))
           - jnp.exp(jnp.sum(lq2_ref[...] * lk2_ref[...], axis=1, keepdims=True)) + lambda_init)

    def scores(r, j, s_scr):
        start = pl.multiple_of(j * tk, tk)
        for a in range(2):
            s_scr[a] = lax.dot_general(k_ref[a, pl.ds(start, tk), :],
                                       q_ref[a, r * tq:(r + 1) * tq, :], NT_DIMS,
                                       preferred_element_type=F32)

    def absorb(j, s_scr, carry):
        vt = vt_ref[j]
        out = []
        for a in range(2):
            m_prev, l_prev = carry[a]
            st = s_scr[a]
            m_new = jnp.maximum(m_prev, jnp.max(st, axis=0, keepdims=True))
            alpha = jnp.exp2(m_prev - m_new)
            p = jnp.exp2(st - m_new)
            l_new = alpha * l_prev + jnp.sum(p, axis=0, keepdims=True)
            acc_scr[a] = alpha * acc_scr[a] + jnp.dot(vt, p.astype(BF16),
                                                      preferred_element_type=F32)
            out.append((m_new, l_new))
        return tuple(out)

    def finalize(r, carry):
        (_, l0), (_, l1) = carry
        ot = acc_scr[0] * (1.0 / l0) - lam * (acc_scr[1] * (1.0 / l1))
        o = ot.T
        o = o * lax.rsqrt(jnp.mean(jnp.square(o), axis=-1, keepdims=True) + RMS_EPS)
        o_ref[r * tq:(r + 1) * tq, :] = (o * sub_ref[...] * (1.0 - lambda_init)).astype(o_ref.dtype)

    init = (jnp.full((1, tq), -jnp.inf, F32), jnp.zeros((1, tq), F32))
    scores(0, 0, s0_scr)
    for r in range(n_blocks):
        acc_scr[...] = jnp.zeros_like(acc_scr)
        carry = (init, init)
        more = r + 1 < n_blocks
        if n_kv > 1:
            def pair(i, carry, r=r):
                scores(r, 2 * i + 1, s1_scr)
                carry = absorb(2 * i, s0_scr, carry)
                scores(r, 2 * i + 2, s0_scr)
                return absorb(2 * i + 1, s1_scr, carry)

            carry = lax.fori_loop(0, n_kv // 2 - 1, pair, carry)
            scores(r, n_kv - 1, s1_scr)
            carry = absorb(n_kv - 2, s0_scr, carry)
            if more:
                scores(r + 1, 0, s0_scr)
            carry = absorb(n_kv - 1, s1_scr, carry)
        else:
            if more:
                scores(r + 1, 0, bufs[(r + 1) % 2])
            carry = absorb(0, bufs[r % 2], carry)
        finalize(r, carry)


def _diff_attention(dqk, dvt, lq1, lk1, lq2, lk2, subln_w, batch, seq, lambda_init):
    t = dqk.shape[1]
    tk = dvt.shape[3]
    n_kv = seq // tk
    assert n_kv == 1 or n_kv % 2 == 0, n_kv
    tq = _tile(seq, ATTN_Q_ROWS)
    rows = _tile(seq, ATTN_Q_ROWS * ATTN_Q_BLOCKS)
    nq = seq // rows
    vec = lambda a: a.reshape(1, -1).astype(F32)
    vspec = lambda n: pl.BlockSpec((1, n), lambda b, h, i: (0, 0))
    return pl.pallas_call(
        functools.partial(_diff_kernel, n_kv=n_kv, tk=tk, tq=tq, lambda_init=lambda_init),
        grid=(batch, DIFF_HEADS, nq),
        in_specs=[
            vspec(DIFF_QK_DIM), vspec(DIFF_QK_DIM), vspec(DIFF_QK_DIM), vspec(DIFF_QK_DIM),
            vspec(DIFF_V_DIM),
            pl.BlockSpec((2, rows, DIFF_QK_DIM), lambda b, h, i: (h, b * nq + i, 0)),
            pl.BlockSpec((2, seq, DIFF_QK_DIM), lambda b, h, i: (DIFF_HEADS + h, b, 0)),
            pl.BlockSpec((None, n_kv, DIFF_V_DIM, tk), lambda b, h, i: (h, b, 0, 0)),
        ],
        out_specs=pl.BlockSpec((rows, DIFF_V_DIM), lambda b, h, i: (b * nq + i, h)),
        out_shape=jax.ShapeDtypeStruct((t, DV_COLS), BF16),
        scratch_shapes=[pltpu.VMEM((2, DIFF_V_DIM, tq), F32),
                        pltpu.VMEM((2, tk, tq), F32),
                        pltpu.VMEM((2, tk, tq), F32)],
        compiler_params=_params(("parallel", "parallel", "arbitrary")),
        name="diff_attention",
    )(vec(lq1), vec(lk1), vec(lq2), vec(lk2), vec(subln_w), dqk, dqk, dvt)


def _outproj_kernel(ro_ref, do_ref, w_ref, x_ref, g_ref, b_ref, o32_ref, o16_ref,
                    mix_scr, pre_scr, *, ni, nj):
    i = pl.program_id(0)
    j = pl.program_id(1)
    half = ro_ref.shape[1]
    rows = o32_ref.shape[0]

    @pl.when(jnp.logical_and(j == 0, i < ni))
    def _():
        mix_scr[:, :half] = ro_ref[...]
        mix_scr[:, half:] = do_ref[...]

    def produce():
        pre_scr[i % 2, j] = DEEPNORM_ALPHA * x_ref[...] + jnp.dot(
            mix_scr[...], w_ref[...], preferred_element_type=F32)

    def normalise():
        slot = (i + 1) % 2
        rs = pl.ds(pl.multiple_of(j * rows, rows), rows)
        pre = jnp.concatenate([pre_scr[slot, c, rs, :] for c in range(nj)], axis=1)
        y = _layer_norm(pre, g_ref[...], b_ref[...])
        o32_ref[...] = y
        o16_ref[...] = y.astype(o16_ref.dtype)

    def both():
        normalise()
        produce()

    _three_phase(i, ni, produce, both, normalise)


def _out_projection(ro, do, w_out, x, ln_g, ln_b):
    t = x.shape[0]
    tm = _tile(t, 512)
    tn = 512
    ni = t // tm
    nj = D_MODEL // tn
    rows = tm // nj
    half = RV_COLS
    tile = lambda i: jnp.minimum(i, ni - 1)
    prev = lambda i, j: (jnp.where(i == 0, 0, (i - 1) * nj + j), 0)
    return pl.pallas_call(
        functools.partial(_outproj_kernel, ni=ni, nj=nj),
        grid=(ni + 1, nj),
        in_specs=[
            pl.BlockSpec((tm, half), lambda i, j: (tile(i), 0)),
            pl.BlockSpec((tm, half), lambda i, j: (tile(i), 0)),
            pl.BlockSpec((2 * half, tn), lambda i, j: (0, j)),
            pl.BlockSpec((tm, tn), lambda i, j: (tile(i), j)),
            pl.BlockSpec((1, D_MODEL), lambda i, j: (0, 0)),
            pl.BlockSpec((1, D_MODEL), lambda i, j: (0, 0)),
        ],
        out_specs=[pl.BlockSpec((rows, D_MODEL), prev), pl.BlockSpec((rows, D_MODEL), prev)],
        out_shape=[jax.ShapeDtypeStruct((t, D_MODEL), F32),
                   jax.ShapeDtypeStruct((t, D_MODEL), BF16)],
        scratch_shapes=[pltpu.VMEM((tm, 2 * half), BF16),
                        pltpu.VMEM((2, nj, tm, tn), F32)],
        compiler_params=_params(("arbitrary", "arbitrary")),
        name="out_projection",
    )(ro, do, w_out, x, ln_g.reshape(1, -1), ln_b.reshape(1, -1))


def _gate_up_kernel(x_ref, wg_ref, wu_ref, o_ref):
    x = x_ref[...]
    g = jnp.dot(x, wg_ref[...], preferred_element_type=F32)
    u = jnp.dot(x, wu_ref[...], preferred_element_type=F32)
    o_ref[...] = (_swish(g) * u).astype(o_ref.dtype)


def _gate_up(xb, wg, wu):
    t, d = xb.shape
    n = wg.shape[1]
    tm = _tile(t, 1024)
    tn = 512
    return pl.pallas_call(
        _gate_up_kernel,
        grid=(t // tm, n // tn),
        in_specs=[pl.BlockSpec((tm, d), lambda i, j: (i, 0)),
                  pl.BlockSpec((d, tn), lambda i, j: (0, j)),
                  pl.BlockSpec((d, tn), lambda i, j: (0, j))],
        out_specs=pl.BlockSpec((tm, tn), lambda i, j: (i, j)),
        out_shape=jax.ShapeDtypeStruct((t, n), BF16),
        compiler_params=_params(("parallel", "arbitrary")),
        name="ffn_gate_up",
    )(xb, wg, wu)


def _down_kernel(h_ref, w_ref, x_ref, g_ref, b_ref, o_ref, acc_scr, *, ni, n_ln):
    i = pl.program_id(0)
    k = pl.program_id(1)
    rows = o_ref.shape[0]
    width = o_ref.shape[1]

    def produce():
        h = h_ref[...]
        slot = i % 2
        for c in range(width // DOWN_COL_CHUNK):
            sl = slice(c * DOWN_COL_CHUNK, (c + 1) * DOWN_COL_CHUNK)
            part = jnp.dot(h, w_ref[:, sl], preferred_element_type=F32)
            acc_scr[slot, :, sl] = jnp.where(k > 0, acc_scr[slot, :, sl], 0.0) + part

    def normalise():
        slot = (i + 1) % 2
        chunk = jnp.minimum(k, n_ln - 1)
        rs = pl.ds(pl.multiple_of(chunk * rows, rows), rows)
        pre = DEEPNORM_ALPHA * x_ref[...] + acc_scr[slot, rs, :]
        o_ref[...] = _layer_norm(pre, g_ref[...], b_ref[...])

    def first():
        @pl.when(k == 0)
        def _():
            acc_scr[...] = jnp.zeros_like(acc_scr)

        produce()

    def both():
        normalise()
        produce()

    _three_phase(i, ni, first, both, normalise)


def _down_projection(h, wd, x1, ln_g, ln_b):
    t, kdim = h.shape
    tm = _tile(t, 512)
    tk = 1024
    ni = t // tm
    nk = kdim // tk
    rows = LN_ROW_CHUNK
    n_ln = tm // rows
    assert n_ln <= nk, (n_ln, nk)
    tile = lambda i: jnp.minimum(i, ni - 1)
    prev = lambda i, k: (jnp.where(i == 0, 0, (i - 1) * n_ln + jnp.minimum(k, n_ln - 1)), 0)
    return pl.pallas_call(
        functools.partial(_down_kernel, ni=ni, n_ln=n_ln),
        grid=(ni + 1, nk),
        in_specs=[pl.BlockSpec((tm, tk), lambda i, k: (tile(i), k)),
                  pl.BlockSpec((tk, D_MODEL), lambda i, k: (k, 0)),
                  pl.BlockSpec((rows, D_MODEL), prev),
                  pl.BlockSpec((1, D_MODEL), lambda i, k: (0, 0)),
                  pl.BlockSpec((1, D_MODEL), lambda i, k: (0, 0))],
        out_specs=pl.BlockSpec((rows, D_MODEL), prev),
        out_shape=jax.ShapeDtypeStruct((t, D_MODEL), F32),
        scratch_shapes=[pltpu.VMEM((2, tm, D_MODEL), F32)],
        compiler_params=_params(("arbitrary", "arbitrary")),
        name="ffn_down",
    )(h, wd, x1, ln_g.reshape(1, -1), ln_b.reshape(1, -1))


def _rope_tables(seq):
    pos = jnp.arange(seq, dtype=F32)

    def cos_sin(rot_dim, base):
        inv_freq = base ** (-jnp.arange(0, rot_dim, 2, dtype=F32) / rot_dim)
        ang = pos[:, None] * inv_freq[None, :]
        return jnp.cos(ang), jnp.sin(ang)

    c, s = cos_sin(RET_QK_DIM, RET_ROT_BASE)
    ks = RET_QK_DIM ** -0.5
    ret_tab = jnp.concatenate([c, c, -s, s, c * ks, c * ks, -s * ks, s * ks], axis=-1)

    c, s = cos_sin(DIFF_ROT_DIM, ROPE_THETA)
    half = DIFF_ROT_DIM // 2
    rest = DIFF_QK_DIM - DIFF_ROT_DIM
    ones = jnp.ones((seq, rest), F32)
    zeros = jnp.zeros((seq, rest), F32)
    zh = jnp.zeros((seq, half), F32)
    qs = DIFF_QK_DIM ** -0.5 * LOG2_E
    diff_tab = jnp.concatenate(
        [c * qs, c * qs, ones * qs, zh, s * qs, zeros, -s * qs, zh, zeros,
         c, c, ones, zh, s, zeros, -s, zh, zeros], axis=-1)
    return ret_tab, diff_tab


def _prepare_weights(w_in, w_out, w_gate, w_up, w_down):
    wb = w_in.astype(BF16)
    c0 = 2 * RQ_COLS
    c1 = c0 + 2 * RV_COLS
    c2 = c1 + 2 * DQ_COLS
    pad = D_FF_PAD - D_FF
    zc = jnp.zeros((w_gate.shape[0], pad), BF16)
    wg = jnp.concatenate([w_gate.astype(BF16), zc], axis=1)
    wu = jnp.concatenate([w_up.astype(BF16), zc], axis=1)
    wd = jnp.concatenate([w_down.astype(BF16), jnp.zeros((pad, w_down.shape[1]), BF16)], axis=0)
    return wb[:, :c0], wb[:, c0:c1], wb[:, c1:c2], wb[:, c2:], w_out.astype(BF16), wg, wu, wd


def _encoder_layer(x, weights, tables, dec_f, dec_b, gn_w, lq1, lk1, lq2, lk2, subln_w,
                   ln1_g, ln1_b, ln2_g, ln2_b, lambda_init):
    batch, seq, d = x.shape
    w_rqk, w_rvg, w_dqk, w_dv, w_out, wg, wu, wd = weights
    x2 = x.reshape(batch * seq, d)
    xb = x2.astype(BF16)
    ret_tab, diff_tab = tables

    rqk = _project(xb, w_rqk, seq, "rope_full", RET_QK_DIM, ret_tab, RQ_COLS)
    rvg = _project(xb, w_rvg, seq, "plain", RET_V_DIM)
    dqk = _project(xb, w_dqk, seq, "rope_partial", DIFF_QK_DIM, diff_tab, DQ_COLS)
    dvt = _project(xb, w_dv, seq, "plain_t", DIFF_V_DIM)

    ro = _retention(rqk, rvg, dec_f, dec_b, gn_w, batch, seq)
    do = _diff_attention(dqk, dvt, lq1, lk1, lq2, lk2, subln_w, batch, seq, lambda_init)

    x1, x1b = _out_projection(ro, do, w_out, x2, ln1_g, ln1_b)
    h = _gate_up(x1b, wg, wu)
    y = _down_projection(h, wd, x1, ln2_g, ln2_b)
    return y.reshape(batch, seq, d)


def kernel(x_prompt, x_sample, w_in, ret_decay_f, ret_decay_b, ret_gn_w, diff_lambda_q1,
           diff_lambda_k1, diff_lambda_q2, diff_lambda_k2, diff_subln_w, w_out, ln1_g, ln1_b,
           w_gate, w_up, w_down, ln2_g, ln2_b):
    y_prompt, y_sample = x_prompt, x_sample
    for l in range(DEPTH):
        lambda_init = 0.8 - 0.6 * math.exp(-0.3 * l)
        weights = _prepare_weights(w_in[l], w_out[l], w_gate[l], w_up[l], w_down[l])
        tables = _rope_tables(max(y_prompt.shape[1], y_sample.shape[1]))
        rest = (tables, ret_decay_f[l], ret_decay_b[l], ret_gn_w[l], diff_lambda_q1[l], diff_lambda_k1[l],
                diff_lambda_q2[l], diff_lambda_k2[l], diff_subln_w[l], ln1_g[l], ln1_b[l],
                ln2_g[l], ln2_b[l], lambda_init)
        y_prompt = _encoder_layer(y_prompt, weights, *rest)
        y_sample = _encoder_layer(y_sample, weights, *rest)
    return (y_prompt, y_sample)
```

```python
import functools
import math

import jax
import jax.numpy as jnp
from jax import lax
from jax.experimental import pallas as pl
from jax.experimental.pallas import tpu as pltpu

D_MODEL = 4096
DEPTH = 1
RET_HEADS = 8
RET_QK_DIM = 128
RET_V_DIM = 256
RET_ROT_BASE = 10000.0
DIFF_HEADS = 8
DIFF_QK_DIM = 128
DIFF_V_DIM = 256
DIFF_ROT_DIM = DIFF_QK_DIM // 4
ROPE_THETA = 500000.0
D_FF = 11008
DEEPNORM_ALPHA = (2.0 * DEPTH) ** 0.25
LN_EPS = 1e-5
GN_EPS = 1e-6
RMS_EPS = 1e-5

RQ_COLS = RET_HEADS * RET_QK_DIM
RV_COLS = RET_HEADS * RET_V_DIM
DQ_COLS = DIFF_HEADS * 2 * DIFF_QK_DIM
DV_COLS = DIFF_HEADS * DIFF_V_DIM

LANES = 128
VMEM_LIMIT_BYTES = 56 * 1024 * 1024
D_FF_PAD = 11264
PROJ_ROWS = 1024
RET_CHUNK = 256
RET_BLOCK = 1024
ATTN_Q_ROWS = 512
ATTN_Q_BLOCKS = 4
DOWN_COL_CHUNK = 2048
LN_ROW_CHUNK = 64
LOG2_E = math.log2(math.e)

BF16 = jnp.bfloat16
F32 = jnp.float32

NT_DIMS = (((1,), (1,)), ((), ()))
TN_DIMS = (((0,), (0,)), ((), ()))


def _tile(n, pref):
    t = pref
    while t > 8 and n % t:
        t //= 2
    assert n % t == 0, (n, pref)
    return t


def _params(sem):
    return pltpu.CompilerParams(dimension_semantics=sem, vmem_limit_bytes=VMEM_LIMIT_BYTES)


def _swish(g):
    return g * (1.0 / (1.0 + jnp.exp(-g)))


def _layer_norm(pre, g, b):
    mu = jnp.mean(pre, axis=-1, keepdims=True)
    var = jnp.mean(jnp.square(pre - mu), axis=-1, keepdims=True)
    return (pre - mu) * lax.rsqrt(var + LN_EPS) * g + b


def _three_phase(i, n, first, middle, last):
    pl.when(i == 0)(first)
    pl.when(jnp.logical_and(i > 0, i < n))(middle)
    pl.when(i == n)(last)


def _proj_kernel(x_ref, w_ref, *rest, mode, hw, cast, scales, split):
    rest = list(rest)
    tab_ref = rest.pop(0) if mode in ("rope_full", "rope_partial") else None
    o_ref = rest.pop(0)
    j = pl.program_id(1)
    if cast:
        lhs_ref = rest.pop(0)

        @pl.when(j == 0)
        def _():
            lhs_ref[...] = x_ref[...].astype(lhs_ref.dtype)
    else:
        lhs_ref = x_ref
    if tab_ref is not None:
        scale = jnp.where(j < split, scales[0], scales[1]).astype(F32)
    tm = x_ref.shape[0]
    n_split = 1 if mode == "plain" else 2
    rows = tm // n_split
    for r in range(n_split):
        rs = slice(r * rows, (r + 1) * rows)
        acc = jnp.dot(lhs_ref[rs, :], w_ref[...], preferred_element_type=F32)
        for c in range(o_ref.shape[0]):
            blk = acc[:, c * hw:(c + 1) * hw]
            if mode == "rope_full":
                cos = tab_ref[rs, 0:LANES]
                sin = tab_ref[rs, LANES:2 * LANES]
                blk = (blk * cos + pltpu.roll(blk, LANES // 2, 1) * sin) * scale
            elif mode == "rope_partial":
                half = DIFF_ROT_DIM // 2
                a = tab_ref[rs, 0:LANES]
                b = tab_ref[rs, LANES:2 * LANES]
                c2 = tab_ref[rs, 2 * LANES:3 * LANES]
                blk = (blk * a + pltpu.roll(blk, half, 1) * b
                       + pltpu.roll(blk, LANES - half, 1) * c2) * scale
            if mode == "plain_t":
                o_ref[c, :, rs] = blk.T.astype(o_ref.dtype)
            else:
                o_ref[c, rs, :] = blk.astype(o_ref.dtype)


def _project(x, w, seq, mode, hw, tab=None, scales=None, split_cols=None, cast=False):
    t, d = x.shape
    n = w.shape[1]
    tm = _tile(seq, PROJ_ROWS // 2 if cast else PROJ_ROWS)
    tn = _tile(n, 512)
    grid = (t // tm, n // tn)
    in_specs = [pl.BlockSpec((tm, d), lambda i, j: (i, 0)),
                pl.BlockSpec((d, tn), lambda i, j: (0, j))]
    args = [x, w]
    if tab is not None:
        per_seq = seq // tm
        in_specs.append(pl.BlockSpec((tm, tab.shape[1]), lambda i, j: (i % per_seq, 0)))
        args.append(tab)
    if mode == "plain_t":
        out_specs = [pl.BlockSpec((tn // hw, None, hw, tm), lambda i, j: (j, i, 0, 0))]
        out_shape = [jax.ShapeDtypeStruct((n // hw, t // tm, hw, tm), BF16)]
    else:
        out_specs = [pl.BlockSpec((tn // hw, tm, hw), lambda i, j: (j, i, 0))]
        out_shape = [jax.ShapeDtypeStruct((n // hw, t, hw), BF16)]
    if cast:
        out_specs.append(pl.BlockSpec((tm, d), lambda i, j: (i, 0)))
        out_shape.append(jax.ShapeDtypeStruct((t, d), BF16))
    out = pl.pallas_call(
        functools.partial(_proj_kernel, mode=mode, hw=hw, cast=cast, scales=scales,
                          split=None if split_cols is None else split_cols // tn),
        grid=grid,
        in_specs=in_specs,
        out_specs=out_specs,
        out_shape=out_shape,
        compiler_params=_params(("parallel", "arbitrary")),
        name="proj_" + mode,
    )(*args)
    return out if cast else out[0]


def _ret_kernel(decf_ref, decb_ref, q_ref, k_ref, v_ref, g_ref, gnw_ref, o_ref,
                sf_scr, sb_scr, sball_scr, *, nblk, nb, chunk):
    s = pl.program_id(2)
    lg_f = -jnp.exp(decf_ref[...])
    lg_b = -jnp.exp(decb_ref[...])
    row = lax.broadcasted_iota(jnp.int32, (chunk, 1), 0).astype(F32)

    @pl.when(s == 0)
    def _():
        sb_scr[...] = jnp.zeros_like(sb_scr)

    @pl.when(s < nblk)
    def _():
        blk = nblk - 1 - s
        zeta_b = jnp.exp(row * lg_b)
        decay_b = jnp.exp(chunk * lg_b)
        state = sb_scr[...]
        for ci in reversed(range(nb)):
            rs = slice(ci * chunk, (ci + 1) * chunk)
            sball_scr[blk * nb + ci] = state.astype(BF16)
            kz = (k_ref[rs, :].astype(F32) * zeta_b).astype(BF16)
            state = decay_b * state + lax.dot_general(kz, v_ref[rs, :], TN_DIMS,
                                                      preferred_element_type=F32)
        sb_scr[...] = state

    @pl.when(s == nblk)
    def _():
        sf_scr[...] = jnp.zeros_like(sf_scr)

    @pl.when(s >= nblk)
    def _():
        blk = s - nblk
        rel = (lax.broadcasted_iota(jnp.int32, (chunk, chunk), 0)
               - lax.broadcasted_iota(jnp.int32, (chunk, chunk), 1)).astype(F32)
        dmat = (jnp.where(rel >= 0, jnp.exp(jnp.maximum(rel, 0.0) * lg_f), 0.0)
                + jnp.where(rel <= 0, jnp.exp(jnp.maximum(-rel, 0.0) * lg_b), 0.0))
        xi_f = jnp.exp((row + 1.0) * lg_f)
        xi_b = jnp.exp((chunk - row) * lg_b)
        zeta_f = jnp.exp((chunk - 1.0 - row) * lg_f)
        decay_f = jnp.exp(chunk * lg_f)
        gnw = gnw_ref[...]
        state = sf_scr[...]
        for ci in range(nb):
            rs = slice(ci * chunk, (ci + 1) * chunk)
            q = q_ref[rs, :]
            k = k_ref[rs, :]
            v = v_ref[rs, :]
            sc = lax.dot_general(q, k, NT_DIMS, preferred_element_type=F32)
            qf = q.astype(F32)
            o = jnp.dot((sc * dmat).astype(BF16), v, preferred_element_type=F32)
            o += jnp.dot((qf * xi_f).astype(BF16), state.astype(BF16), preferred_element_type=F32)
            o += jnp.dot((qf * xi_b).astype(BF16), sball_scr[blk * nb + ci],
                         preferred_element_type=F32)
            kz = (k.astype(F32) * zeta_f).astype(BF16)
            state = decay_f * state + lax.dot_general(kz, v, TN_DIMS, preferred_element_type=F32)
            mu = jnp.mean(o, axis=-1, keepdims=True)
            var = jnp.mean(jnp.square(o - mu), axis=-1, keepdims=True)
            on = (o - mu) * lax.rsqrt(var + GN_EPS) * gnw
            o_ref[rs, :] = (_swish(g_ref[rs, :].astype(F32)) * on).astype(o_ref.dtype)
        sf_scr[...] = state


def _retention(rqk, rvg, dec_f, dec_b, gn_w, batch, seq):
    t = rqk.shape[1]
    rows = _tile(seq, RET_BLOCK)
    chunk = _tile(rows, RET_CHUNK)
    nb = rows // chunk
    nblk = seq // rows

    def bidx(b, s):
        return b * nblk + jnp.where(s < nblk, nblk - 1 - s, s - nblk)

    def oidx(b, s):
        return b * nblk + jnp.maximum(s - nblk, 0)

    return pl.pallas_call(
        functools.partial(_ret_kernel, nblk=nblk, nb=nb, chunk=chunk),
        grid=(batch, RET_HEADS, 2 * nblk),
        in_specs=[
            pl.BlockSpec((None, 1, 1), lambda b, h, s: (h, 0, 0)),
            pl.BlockSpec((None, 1, 1), lambda b, h, s: (h, 0, 0)),
            pl.BlockSpec((None, rows, RET_QK_DIM), lambda b, h, s: (h, oidx(b, s), 0)),
            pl.BlockSpec((None, rows, RET_QK_DIM), lambda b, h, s: (RET_HEADS + h, bidx(b, s), 0)),
            pl.BlockSpec((None, rows, RET_V_DIM), lambda b, h, s: (h, bidx(b, s), 0)),
            pl.BlockSpec((None, rows, RET_V_DIM), lambda b, h, s: (RET_HEADS + h, oidx(b, s), 0)),
            pl.BlockSpec((None, 1, RET_V_DIM), lambda b, h, s: (h, 0, 0)),
        ],
        out_specs=pl.BlockSpec((rows, RET_V_DIM), lambda b, h, s: (oidx(b, s), h)),
        out_shape=jax.ShapeDtypeStruct((t, RV_COLS), BF16),
        scratch_shapes=[
            pltpu.VMEM((RET_QK_DIM, RET_V_DIM), F32),
            pltpu.VMEM((RET_QK_DIM, RET_V_DIM), F32),
            pltpu.VMEM((nblk * nb, RET_QK_DIM, RET_V_DIM), BF16),
        ],
        compiler_params=_params(("parallel", "parallel", "arbitrary")),
        name="retention",
    )(dec_f.reshape(RET_HEADS, 1, 1), dec_b.reshape(RET_HEADS, 1, 1),
      rqk, rqk, rvg, rvg, gn_w.reshape(RET_HEADS, 1, RET_V_DIM))


def _diff_kernel(lq1_ref, lk1_ref, lq2_ref, lk2_ref, sub_ref, q_ref, k_ref, vt_ref, o_ref,
                 acc_scr, s0_scr, s1_scr, m0_scr, m1_scr, *, n_kv, tk, tq, lambda_init):
    n_blocks = q_ref.shape[1] // tq
    s0 = (s0_scr, m0_scr)
    s1 = (s1_scr, m1_scr)
    bufs = (s0, s1)
    lam = (jnp.exp(jnp.sum(lq1_ref[...] * lk1_ref[...], axis=1, keepdims=True))
           - jnp.exp(jnp.sum(lq2_ref[...] * lk2_ref[...], axis=1, keepdims=True)) + lambda_init)

    def scores(r, j, buf):
        s_scr, m_scr = buf
        start = pl.multiple_of(j * tk, tk)
        for a in range(2):
            st = lax.dot_general(k_ref[a, pl.ds(start, tk), :],
                                 q_ref[a, r * tq:(r + 1) * tq, :], NT_DIMS,
                                 preferred_element_type=F32)
            s_scr[a] = st
            m_scr[a] = jnp.max(st, axis=0, keepdims=True)

    def absorb(j, buf, carry):
        s_scr, m_scr = buf
        vt = vt_ref[j]
        out = []
        for a in range(2):
            m_prev, l_prev = carry[a]
            st = s_scr[a]
            m_new = jnp.maximum(m_prev, m_scr[a])
            alpha = jnp.exp2(m_prev - m_new)
            p = jnp.exp2(st - m_new)
            l_new = alpha * l_prev + jnp.sum(p, axis=0, keepdims=True)
            acc_scr[a] = alpha * acc_scr[a] + jnp.dot(vt, p.astype(BF16),
                                                      preferred_element_type=F32)
            out.append((m_new, l_new))
        return tuple(out)

    def finalize(r, carry):
        (_, l0), (_, l1) = carry
        ot = acc_scr[0] * (1.0 / l0) - lam * (acc_scr[1] * (1.0 / l1))
        o = ot.T
        o = o * lax.rsqrt(jnp.mean(jnp.square(o), axis=-1, keepdims=True) + RMS_EPS)
        o_ref[r * tq:(r + 1) * tq, :] = (o * sub_ref[...] * (1.0 - lambda_init)).astype(o_ref.dtype)

    init = (jnp.full((1, tq), -jnp.inf, F32), jnp.zeros((1, tq), F32))
    scores(0, 0, s0)
    for r in range(n_blocks):
        acc_scr[...] = jnp.zeros_like(acc_scr)
        carry = (init, init)
        more = r + 1 < n_blocks
        if n_kv > 1:
            def pair(i, carry, r=r):
                scores(r, 2 * i + 1, s1)
                carry = absorb(2 * i, s0, carry)
                scores(r, 2 * i + 2, s0)
                return absorb(2 * i + 1, s1, carry)

            carry = lax.fori_loop(0, n_kv // 2 - 1, pair, carry)
            scores(r, n_kv - 1, s1)
            carry = absorb(n_kv - 2, s0, carry)
            if more:
                scores(r + 1, 0, s0)
            carry = absorb(n_kv - 1, s1, carry)
        else:
            if more:
                scores(r + 1, 0, bufs[(r + 1) % 2])
            carry = absorb(0, bufs[r % 2], carry)
        finalize(r, carry)


def _diff_attention(dqk, dvt, lq1, lk1, lq2, lk2, subln_w, batch, seq, lambda_init):
    t = dqk.shape[1]
    tk = dvt.shape[3]
    n_kv = seq // tk
    assert n_kv == 1 or n_kv % 2 == 0, n_kv
    tq = _tile(seq, ATTN_Q_ROWS)
    rows = _tile(seq, ATTN_Q_ROWS * ATTN_Q_BLOCKS)
    nq = seq // rows
    vec = lambda a: a.reshape(1, -1).astype(F32)
    vspec = lambda n: pl.BlockSpec((1, n), lambda b, h, i: (0, 0))
    return pl.pallas_call(
        functools.partial(_diff_kernel, n_kv=n_kv, tk=tk, tq=tq, lambda_init=lambda_init),
        grid=(batch, DIFF_HEADS, nq),
        in_specs=[
            vspec(DIFF_QK_DIM), vspec(DIFF_QK_DIM), vspec(DIFF_QK_DIM), vspec(DIFF_QK_DIM),
            vspec(DIFF_V_DIM),
            pl.BlockSpec((2, rows, DIFF_QK_DIM), lambda b, h, i: (h, b * nq + i, 0)),
            pl.BlockSpec((2, seq, DIFF_QK_DIM), lambda b, h, i: (DIFF_HEADS + h, b, 0)),
            pl.BlockSpec((None, n_kv, DIFF_V_DIM, tk), lambda b, h, i: (h, b, 0, 0)),
        ],
        out_specs=pl.BlockSpec((rows, DIFF_V_DIM), lambda b, h, i: (b * nq + i, h)),
        out_shape=jax.ShapeDtypeStruct((t, DV_COLS), BF16),
        scratch_shapes=[pltpu.VMEM((2, DIFF_V_DIM, tq), F32),
                        pltpu.VMEM((2, tk, tq), F32),
                        pltpu.VMEM((2, tk, tq), F32),
                        pltpu.VMEM((2, 1, tq), F32),
                        pltpu.VMEM((2, 1, tq), F32)],
        compiler_params=_params(("parallel", "parallel", "arbitrary")),
        name="diff_attention",
    )(vec(lq1), vec(lk1), vec(lq2), vec(lk2), vec(subln_w), dqk, dqk, dvt)


def _outproj_kernel(ro_ref, do_ref, w_ref, x_ref, g_ref, b_ref, o32_ref, o16_ref,
                    mix_scr, pre_scr, *, ni, nj):
    i = pl.program_id(0)
    j = pl.program_id(1)
    half = ro_ref.shape[1]
    rows = o32_ref.shape[0]

    @pl.when(jnp.logical_and(j == 0, i < ni))
    def _():
        mix_scr[:, :half] = ro_ref[...]
        mix_scr[:, half:] = do_ref[...]

    def produce():
        pre_scr[i % 2, j] = DEEPNORM_ALPHA * x_ref[...] + jnp.dot(
            mix_scr[...], w_ref[...], preferred_element_type=F32)

    def normalise():
        slot = (i + 1) % 2
        rs = pl.ds(pl.multiple_of(j * rows, rows), rows)
        pre = jnp.concatenate([pre_scr[slot, c, rs, :] for c in range(nj)], axis=1)
        y = _layer_norm(pre, g_ref[...], b_ref[...])
        o32_ref[...] = y
        o16_ref[...] = y.astype(o16_ref.dtype)

    def both():
        normalise()
        produce()

    _three_phase(i, ni, produce, both, normalise)


def _out_projection(ro, do, w_out, x, ln_g, ln_b):
    t = x.shape[0]
    tm = _tile(t, 512)
    tn = 512
    ni = t // tm
    nj = D_MODEL // tn
    rows = tm // nj
    half = RV_COLS
    tile = lambda i: jnp.minimum(i, ni - 1)
    prev = lambda i, j: (jnp.where(i == 0, 0, (i - 1) * nj + j), 0)
    return pl.pallas_call(
        functools.partial(_outproj_kernel, ni=ni, nj=nj),
        grid=(ni + 1, nj),
        in_specs=[
            pl.BlockSpec((tm, half), lambda i, j: (tile(i), 0)),
            pl.BlockSpec((tm, half), lambda i, j: (tile(i), 0)),
            pl.BlockSpec((2 * half, tn), lambda i, j: (0, j)),
            pl.BlockSpec((tm, tn), lambda i, j: (tile(i), j)),
            pl.BlockSpec((1, D_MODEL), lambda i, j: (0, 0)),
            pl.BlockSpec((1, D_MODEL), lambda i, j: (0, 0)),
        ],
        out_specs=[pl.BlockSpec((rows, D_MODEL), prev), pl.BlockSpec((rows, D_MODEL), prev)],
        out_shape=[jax.ShapeDtypeStruct((t, D_MODEL), F32),
                   jax.ShapeDtypeStruct((t, D_MODEL), BF16)],
        scratch_shapes=[pltpu.VMEM((tm, 2 * half), BF16),
                        pltpu.VMEM((2, nj, tm, tn), F32)],
        compiler_params=_params(("arbitrary", "arbitrary")),
        name="out_projection",
    )(ro, do, w_out, x, ln_g.reshape(1, -1), ln_b.reshape(1, -1))


def _gate_up_kernel(x_ref, wg_ref, wu_ref, o_ref):
    x = x_ref[...]
    g = jnp.dot(x, wg_ref[...], preferred_element_type=F32)
    u = jnp.dot(x, wu_ref[...], preferred_element_type=F32)
    o_ref[...] = (_swish(g) * u).astype(o_ref.dtype)


def _gate_up(xb, wg, wu):
    t, d = xb.shape
    n = wg.shape[1]
    tm = _tile(t, 1024)
    tn = 512
    return pl.pallas_call(
        _gate_up_kernel,
        grid=(t // tm, n // tn),
        in_specs=[pl.BlockSpec((tm, d), lambda i, j: (i, 0)),
                  pl.BlockSpec((d, tn), lambda i, j: (0, j)),
                  pl.BlockSpec((d, tn), lambda i, j: (0, j))],
        out_specs=pl.BlockSpec((tm, tn), lambda i, j: (i, j)),
        out_shape=jax.ShapeDtypeStruct((t, n), BF16),
        compiler_params=_params(("parallel", "arbitrary")),
        name="ffn_gate_up",
    )(xb, wg, wu)


def _down_kernel(h_ref, w_ref, x_ref, g_ref, b_ref, o_ref, acc_scr, *, ni, n_ln):
    i = pl.program_id(0)
    k = pl.program_id(1)
    rows = o_ref.shape[0]
    width = o_ref.shape[1]

    def produce():
        h = h_ref[...]
        slot = i % 2
        for c in range(width // DOWN_COL_CHUNK):
            sl = slice(c * DOWN_COL_CHUNK, (c + 1) * DOWN_COL_CHUNK)
            part = jnp.dot(h, w_ref[:, sl], preferred_element_type=F32)
            acc_scr[slot, :, sl] = jnp.where(k > 0, acc_scr[slot, :, sl], 0.0) + part

    def normalise():
        slot = (i + 1) % 2
        chunk = jnp.minimum(k, n_ln - 1)
        rs = pl.ds(pl.multiple_of(chunk * rows, rows), rows)
        pre = DEEPNORM_ALPHA * x_ref[...] + acc_scr[slot, rs, :]
        o_ref[...] = _layer_norm(pre, g_ref[...], b_ref[...])

    def first():
        @pl.when(k == 0)
        def _():
            acc_scr[...] = jnp.zeros_like(acc_scr)

        produce()

    def both():
        normalise()
        produce()

    _three_phase(i, ni, first, both, normalise)


def _down_projection(h, wd, x1, ln_g, ln_b):
    t, kdim = h.shape
    tm = _tile(t, 512)
    tk = 1024
    ni = t // tm
    nk = kdim // tk
    rows = LN_ROW_CHUNK
    n_ln = tm // rows
    assert n_ln <= nk, (n_ln, nk)
    tile = lambda i: jnp.minimum(i, ni - 1)
    prev = lambda i, k: (jnp.where(i == 0, 0, (i - 1) * n_ln + jnp.minimum(k, n_ln - 1)), 0)
    return pl.pallas_call(
        functools.partial(_down_kernel, ni=ni, n_ln=n_ln),
        grid=(ni + 1, nk),
        in_specs=[pl.BlockSpec((tm, tk), lambda i, k: (tile(i), k)),
                  pl.BlockSpec((tk, D_MODEL), lambda i, k: (k, 0)),
                  pl.BlockSpec((rows, D_MODEL), prev),
                  pl.BlockSpec((1, D_MODEL), lambda i, k: (0, 0)),
                  pl.BlockSpec((1, D_MODEL), lambda i, k: (0, 0))],
        out_specs=pl.BlockSpec((rows, D_MODEL), prev),
        out_shape=jax.ShapeDtypeStruct((t, D_MODEL), F32),
        scratch_shapes=[pltpu.VMEM((2, tm, D_MODEL), F32)],
        compiler_params=_params(("arbitrary", "arbitrary")),
        name="ffn_down",
    )(h, wd, x1, ln_g.reshape(1, -1), ln_b.reshape(1, -1))


def _rope_tables(seq):
    pos = jnp.arange(seq, dtype=F32)

    def cos_sin(rot_dim, base):
        inv_freq = base ** (-jnp.arange(0, rot_dim, 2, dtype=F32) / rot_dim)
        ang = pos[:, None] * inv_freq[None, :]
        return jnp.cos(ang), jnp.sin(ang)

    c, s = cos_sin(RET_QK_DIM, RET_ROT_BASE)
    ret_tab = jnp.concatenate([c, c, -s, s], axis=-1)

    c, s = cos_sin(DIFF_ROT_DIM, ROPE_THETA)
    half = DIFF_ROT_DIM // 2
    rest = DIFF_QK_DIM - DIFF_ROT_DIM
    ones = jnp.ones((seq, rest), F32)
    zeros = jnp.zeros((seq, rest), F32)
    zh = jnp.zeros((seq, half), F32)
    diff_tab = jnp.concatenate([c, c, ones, zh, s, zeros, -s, zh, zeros], axis=-1)
    return ret_tab, diff_tab


def _prepare_weights(w_in, w_out, w_gate, w_up, w_down):
    wb = w_in.astype(BF16)
    c0 = 2 * RQ_COLS
    c1 = c0 + 2 * RV_COLS
    c2 = c1 + 2 * DQ_COLS
    pad = D_FF_PAD - D_FF
    zc = jnp.zeros((w_gate.shape[0], pad), BF16)
    wg = jnp.concatenate([w_gate.astype(BF16), zc], axis=1)
    wu = jnp.concatenate([w_up.astype(BF16), zc], axis=1)
    wd = jnp.concatenate([w_down.astype(BF16), jnp.zeros((pad, w_down.shape[1]), BF16)], axis=0)
    return wb[:, :c0], wb[:, c0:c1], wb[:, c1:c2], wb[:, c2:], w_out.astype(BF16), wg, wu, wd


def _encoder_layer(x, weights, tables, dec_f, dec_b, gn_w, lq1, lk1, lq2, lk2, subln_w,
                   ln1_g, ln1_b, ln2_g, ln2_b, lambda_init):
    batch, seq, d = x.shape
    w_rqk, w_rvg, w_dqk, w_dv, w_out, wg, wu, wd = weights
    x2 = x.reshape(batch * seq, d)
    ret_tab, diff_tab = tables

    rqk, xb = _project(x2, w_rqk, seq, "rope_full", RET_QK_DIM, ret_tab,
                       (1.0, RET_QK_DIM ** -0.5), RQ_COLS, cast=True)
    rvg = _project(xb, w_rvg, seq, "plain", RET_V_DIM)
    dqk = _project(xb, w_dqk, seq, "rope_partial", DIFF_QK_DIM, diff_tab,
                   (DIFF_QK_DIM ** -0.5 * LOG2_E, 1.0), DQ_COLS)
    dvt = _project(xb, w_dv, seq, "plain_t", DIFF_V_DIM)

    ro = _retention(rqk, rvg, dec_f, dec_b, gn_w, batch, seq)
    do = _diff_attention(dqk, dvt, lq1, lk1, lq2, lk2, subln_w, batch, seq, lambda_init)

    x1, x1b = _out_projection(ro, do, w_out, x2, ln1_g, ln1_b)
    h = _gate_up(x1b, wg, wu)
    y = _down_projection(h, wd, x1, ln2_g, ln2_b)
    return y.reshape(batch, seq, d)


def kernel(x_prompt, x_sample, w_in, ret_decay_f, ret_decay_b, ret_gn_w, diff_lambda_q1,
           diff_lambda_k1, diff_lambda_q2, diff_lambda_k2, diff_subln_w, w_out, ln1_g, ln1_b,
           w_gate, w_up, w_down, ln2_g, ln2_b):
    y_prompt, y_sample = x_prompt, x_sample
    for l in range(DEPTH):
        lambda_init = 0.8 - 0.6 * math.exp(-0.3 * l)
        weights = _prepare_weights(w_in[l], w_out[l], w_gate[l], w_up[l], w_down[l])
        tables = _rope_tables(max(y_prompt.shape[1], y_sample.shape[1]))
        rest = (tables, ret_decay_f[l], ret_decay_b[l], ret_gn_w[l], diff_lambda_q1[l], diff_lambda_k1[l],
                diff_lambda_q2[l], diff_lambda_k2[l], diff_subln_w[l], ln1_g[l], ln1_b[l],
                ln2_g[l], ln2_b[l], lambda_init)
        y_prompt = _encoder_layer(y_prompt, weights, *rest)
        y_sample = _encoder_layer(y_sample, weights, *rest)
    return (y_prompt, y_sample)
```

```python
import functools
import math

import jax
import jax.numpy as jnp
from jax import lax
from jax.experimental import pallas as pl
from jax.experimental.pallas import tpu as pltpu

D_MODEL = 4096
DEPTH = 1
RET_HEADS = 8
RET_QK_DIM = 128
RET_V_DIM = 256
RET_ROT_BASE = 10000.0
DIFF_HEADS = 8
DIFF_QK_DIM = 128
DIFF_V_DIM = 256
DIFF_ROT_DIM = DIFF_QK_DIM // 4
ROPE_THETA = 500000.0
D_FF = 11008
DEEPNORM_ALPHA = (2.0 * DEPTH) ** 0.25
LN_EPS = 1e-5
GN_EPS = 1e-6
RMS_EPS = 1e-5

RQ_COLS = RET_HEADS * RET_QK_DIM
RV_COLS = RET_HEADS * RET_V_DIM
DQ_COLS = DIFF_HEADS * 2 * DIFF_QK_DIM
DV_COLS = DIFF_HEADS * DIFF_V_DIM

LANES = 128
VMEM_LIMIT_BYTES = 56 * 1024 * 1024
D_FF_PAD = 11264
WEIGHT_COLS = 512
PROJ_ROWS = 1024
RET_CHUNK = 256
RET_BLOCK = 1024
ATTN_Q_ROWS = 512
ATTN_Q_BLOCKS = 4
DOWN_COL_CHUNK = 4096
LN_ROW_CHUNK = 64
LOG2_E = math.log2(math.e)

BF16 = jnp.bfloat16
F32 = jnp.float32

NT_DIMS = (((1,), (1,)), ((), ()))
TN_DIMS = (((0,), (0,)), ((), ()))


def _tile(n, pref):
    t = pref
    while t > 8 and n % t:
        t //= 2
    assert n % t == 0, (n, pref)
    return t


def _params(sem):
    return pltpu.CompilerParams(dimension_semantics=sem, vmem_limit_bytes=VMEM_LIMIT_BYTES)


def _swish(g):
    return g * (1.0 / (1.0 + jnp.exp(-g)))


def _layer_norm(pre, g, b):
    mu = jnp.mean(pre, axis=-1, keepdims=True)
    var = jnp.mean(jnp.square(pre - mu), axis=-1, keepdims=True)
    return (pre - mu) * lax.rsqrt(var + LN_EPS) * g + b


def _three_phase(i, n, first, middle, last):
    pl.when(i == 0)(first)
    pl.when(jnp.logical_and(i > 0, i < n))(middle)
    pl.when(i == n)(last)


def _proj_kernel(x_ref, w_ref, *rest, mode, hw, cast, scales, split):
    rest = list(rest)
    tab_ref = rest.pop(0) if mode in ("rope_full", "rope_partial") else None
    o_ref = rest.pop(0)
    j = pl.program_id(1)
    if cast:
        lhs_ref = rest.pop(0)

        @pl.when(j == 0)
        def _():
            lhs_ref[...] = x_ref[...].astype(lhs_ref.dtype)
    else:
        lhs_ref = x_ref
    if tab_ref is not None:
        scale = jnp.where(j < split, scales[0], scales[1]).astype(F32)
    tm = x_ref.shape[0]
    n_split = 1 if mode == "plain" else 2
    rows = tm // n_split
    for r in range(n_split):
        rs = slice(r * rows, (r + 1) * rows)
        acc = jnp.dot(lhs_ref[rs, :], w_ref[...], preferred_element_type=F32)
        for c in range(o_ref.shape[0]):
            blk = acc[:, c * hw:(c + 1) * hw]
            if mode == "rope_full":
                cos = tab_ref[rs, 0:LANES]
                sin = tab_ref[rs, LANES:2 * LANES]
                blk = (blk * cos + pltpu.roll(blk, LANES // 2, 1) * sin) * scale
            elif mode == "rope_partial":
                half = DIFF_ROT_DIM // 2
                a = tab_ref[rs, 0:LANES]
                b = tab_ref[rs, LANES:2 * LANES]
                c2 = tab_ref[rs, 2 * LANES:3 * LANES]
                blk = (blk * a + pltpu.roll(blk, half, 1) * b
                       + pltpu.roll(blk, LANES - half, 1) * c2) * scale
            if mode == "plain_t":
                o_ref[c, :, rs] = blk.T.astype(o_ref.dtype)
            else:
                o_ref[c, rs, :] = blk.astype(o_ref.dtype)


def _project(x, w, seq, mode, hw, tab=None, scales=None, split_cols=None, cast=False):
    t, d = x.shape
    n_tiles, _, tn = w.shape
    n = n_tiles * tn
    tm = _tile(seq, PROJ_ROWS // 2 if cast else PROJ_ROWS)
    grid = (t // tm, n_tiles)
    in_specs = [pl.BlockSpec((tm, d), lambda i, j: (i, 0)),
                pl.BlockSpec((None, d, tn), lambda i, j: (j, 0, 0))]
    args = [x, w]
    if tab is not None:
        per_seq = seq // tm
        in_specs.append(pl.BlockSpec((tm, tab.shape[1]), lambda i, j: (i % per_seq, 0)))
        args.append(tab)
    if mode == "plain_t":
        out_specs = [pl.BlockSpec((tn // hw, None, hw, tm), lambda i, j: (j, i, 0, 0))]
        out_shape = [jax.ShapeDtypeStruct((n // hw, t // tm, hw, tm), BF16)]
    else:
        out_specs = [pl.BlockSpec((tn // hw, tm, hw), lambda i, j: (j, i, 0))]
        out_shape = [jax.ShapeDtypeStruct((n // hw, t, hw), BF16)]
    if cast:
        out_specs.append(pl.BlockSpec((tm, d), lambda i, j: (i, 0)))
        out_shape.append(jax.ShapeDtypeStruct((t, d), BF16))
    out = pl.pallas_call(
        functools.partial(_proj_kernel, mode=mode, hw=hw, cast=cast, scales=scales,
                          split=None if split_cols is None else split_cols // tn),
        grid=grid,
        in_specs=in_specs,
        out_specs=out_specs,
        out_shape=out_shape,
        compiler_params=_params(("parallel", "arbitrary")),
        name="proj_" + mode,
    )(*args)
    return out if cast else out[0]


def _ret_kernel(decf_ref, decb_ref, q_ref, k_ref, v_ref, g_ref, gnw_ref, o_ref,
                sf_scr, sb_scr, sball_scr, *, nblk, nb, chunk):
    s = pl.program_id(2)
    lg_f = -jnp.exp(decf_ref[...])
    lg_b = -jnp.exp(decb_ref[...])
    row = lax.broadcasted_iota(jnp.int32, (chunk, 1), 0).astype(F32)

    @pl.when(s == 0)
    def _():
        sb_scr[...] = jnp.zeros_like(sb_scr)

    @pl.when(s < nblk)
    def _():
        blk = nblk - 1 - s
        zeta_b = jnp.exp(row * lg_b)
        decay_b = jnp.exp(chunk * lg_b)
        state = sb_scr[...]
        for ci in reversed(range(nb)):
            rs = slice(ci * chunk, (ci + 1) * chunk)
            sball_scr[blk * nb + ci] = state.astype(BF16)
            kz = (k_ref[rs, :].astype(F32) * zeta_b).astype(BF16)
            state = decay_b * state + lax.dot_general(kz, v_ref[rs, :], TN_DIMS,
                                                      preferred_element_type=F32)
        sb_scr[...] = state

    @pl.when(s == nblk)
    def _():
        sf_scr[...] = jnp.zeros_like(sf_scr)

    @pl.when(s >= nblk)
    def _():
        blk = s - nblk
        rel = (lax.broadcasted_iota(jnp.int32, (chunk, chunk), 0)
               - lax.broadcasted_iota(jnp.int32, (chunk, chunk), 1)).astype(F32)
        dmat = (jnp.where(rel >= 0, jnp.exp(jnp.maximum(rel, 0.0) * lg_f), 0.0)
                + jnp.where(rel <= 0, jnp.exp(jnp.maximum(-rel, 0.0) * lg_b), 0.0))
        xi_f = jnp.exp((row + 1.0) * lg_f)
        xi_b = jnp.exp((chunk - row) * lg_b)
        zeta_f = jnp.exp((chunk - 1.0 - row) * lg_f)
        decay_f = jnp.exp(chunk * lg_f)
        gnw = gnw_ref[...]
        state = sf_scr[...]
        for ci in range(nb):
            rs = slice(ci * chunk, (ci + 1) * chunk)
            q = q_ref[rs, :]
            k = k_ref[rs, :]
            v = v_ref[rs, :]
            sc = lax.dot_general(q, k, NT_DIMS, preferred_element_type=F32)
            qf = q.astype(F32)
            o = jnp.dot((sc * dmat).astype(BF16), v, preferred_element_type=F32)
            o += jnp.dot((qf * xi_f).astype(BF16), state.astype(BF16), preferred_element_type=F32)
            o += jnp.dot((qf * xi_b).astype(BF16), sball_scr[blk * nb + ci],
                         preferred_element_type=F32)
            kz = (k.astype(F32) * zeta_f).astype(BF16)
            state = decay_f * state + lax.dot_general(kz, v, TN_DIMS, preferred_element_type=F32)
            mu = jnp.mean(o, axis=-1, keepdims=True)
            var = jnp.mean(jnp.square(o - mu), axis=-1, keepdims=True)
            on = (o - mu) * lax.rsqrt(var + GN_EPS) * gnw
            o_ref[rs, :] = (_swish(g_ref[rs, :].astype(F32)) * on).astype(o_ref.dtype)
        sf_scr[...] = state


def _retention(rqk, rvg, dec_f, dec_b, gn_w, batch, seq):
    t = rqk.shape[1]
    rows = _tile(seq, RET_BLOCK)
    chunk = _tile(rows, RET_CHUNK)
    nb = rows // chunk
    nblk = seq // rows

    def bidx(b, s):
        return b * nblk + jnp.where(s < nblk, nblk - 1 - s, s - nblk)

    def oidx(b, s):
        return b * nblk + jnp.maximum(s - nblk, 0)

    return pl.pallas_call(
        functools.partial(_ret_kernel, nblk=nblk, nb=nb, chunk=chunk),
        grid=(batch, RET_HEADS, 2 * nblk),
        in_specs=[
            pl.BlockSpec((None, 1, 1), lambda b, h, s: (h, 0, 0)),
            pl.BlockSpec((None, 1, 1), lambda b, h, s: (h, 0, 0)),
            pl.BlockSpec((None, rows, RET_QK_DIM), lambda b, h, s: (h, oidx(b, s), 0)),
            pl.BlockSpec((None, rows, RET_QK_DIM), lambda b, h, s: (RET_HEADS + h, bidx(b, s), 0)),
            pl.BlockSpec((None, rows, RET_V_DIM), lambda b, h, s: (h, bidx(b, s), 0)),
            pl.BlockSpec((None, rows, RET_V_DIM), lambda b, h, s: (RET_HEADS + h, oidx(b, s), 0)),
            pl.BlockSpec((None, 1, RET_V_DIM), lambda b, h, s: (h, 0, 0)),
        ],
        out_specs=pl.BlockSpec((rows, RET_V_DIM), lambda b, h, s: (oidx(b, s), h)),
        out_shape=jax.ShapeDtypeStruct((t, RV_COLS), BF16),
        scratch_shapes=[
            pltpu.VMEM((RET_QK_DIM, RET_V_DIM), F32),
            pltpu.VMEM((RET_QK_DIM, RET_V_DIM), F32),
            pltpu.VMEM((nblk * nb, RET_QK_DIM, RET_V_DIM), BF16),
        ],
        compiler_params=_params(("parallel", "parallel", "arbitrary")),
        name="retention",
    )(dec_f.reshape(RET_HEADS, 1, 1), dec_b.reshape(RET_HEADS, 1, 1),
      rqk, rqk, rvg, rvg, gn_w.reshape(RET_HEADS, 1, RET_V_DIM))


def _diff_kernel(lq1_ref, lk1_ref, lq2_ref, lk2_ref, sub_ref, q_ref, k_ref, vt_ref, o_ref,
                 acc_scr, s0_scr, s1_scr, m0_scr, m1_scr, *, n_kv, tk, tq, lambda_init):
    n_blocks = q_ref.shape[1] // tq
    s0 = (s0_scr, m0_scr)
    s1 = (s1_scr, m1_scr)
    bufs = (s0, s1)
    lam = (jnp.exp(jnp.sum(lq1_ref[...] * lk1_ref[...], axis=1, keepdims=True))
           - jnp.exp(jnp.sum(lq2_ref[...] * lk2_ref[...], axis=1, keepdims=True)) + lambda_init)

    def scores(r, j, buf):
        s_scr, m_scr = buf
        start = pl.multiple_of(j * tk, tk)
        for a in range(2):
            st = lax.dot_general(k_ref[a, pl.ds(start, tk), :],
                                 q_ref[a, r * tq:(r + 1) * tq, :], NT_DIMS,
                                 preferred_element_type=F32)
            s_scr[a] = st
            m_scr[a] = jnp.max(st, axis=0, keepdims=True)

    def absorb(j, buf, carry):
        s_scr, m_scr = buf
        vt = vt_ref[j]
        out = []
        for a in range(2):
            m_prev, l_prev = carry[a]
            st = s_scr[a]
            m_new = jnp.maximum(m_prev, m_scr[a])
            alpha = jnp.exp2(m_prev - m_new)
            p = jnp.exp2(st - m_new)
            l_new = alpha * l_prev + jnp.sum(p, axis=0, keepdims=True)
            acc_scr[a] = alpha * acc_scr[a] + jnp.dot(vt, p.astype(BF16),
                                                      preferred_element_type=F32)
            out.append((m_new, l_new))
        return tuple(out)

    def finalize(r, carry):
        (_, l0), (_, l1) = carry
        ot = acc_scr[0] * (1.0 / l0) - lam * (acc_scr[1] * (1.0 / l1))
        o = ot.T
        o = o * lax.rsqrt(jnp.mean(jnp.square(o), axis=-1, keepdims=True) + RMS_EPS)
        o_ref[r * tq:(r + 1) * tq, :] = (o * sub_ref[...] * (1.0 - lambda_init)).astype(o_ref.dtype)

    init = (jnp.full((1, tq), -jnp.inf, F32), jnp.zeros((1, tq), F32))
    scores(0, 0, s0)
    for r in range(n_blocks):
        acc_scr[...] = jnp.zeros_like(acc_scr)
        carry = (init, init)
        more = r + 1 < n_blocks
        if n_kv > 1:
            def pair(i, carry, r=r):
                scores(r, 2 * i + 1, s1)
                carry = absorb(2 * i, s0, carry)
                scores(r, 2 * i + 2, s0)
                return absorb(2 * i + 1, s1, carry)

            carry = lax.fori_loop(0, n_kv // 2 - 1, pair, carry)
            scores(r, n_kv - 1, s1)
            carry = absorb(n_kv - 2, s0, carry)
            if more:
                scores(r + 1, 0, s0)
            carry = absorb(n_kv - 1, s1, carry)
        else:
            if more:
                scores(r + 1, 0, bufs[(r + 1) % 2])
            carry = absorb(0, bufs[r % 2], carry)
        finalize(r, carry)


def _diff_attention(dqk, dvt, lq1, lk1, lq2, lk2, subln_w, batch, seq, lambda_init):
    t = dqk.shape[1]
    tk = dvt.shape[3]
    n_kv = seq // tk
    assert n_kv == 1 or n_kv % 2 == 0, n_kv
    tq = _tile(seq, ATTN_Q_ROWS)
    rows = _tile(seq, ATTN_Q_ROWS * ATTN_Q_BLOCKS)
    nq = seq // rows
    vec = lambda a: a.reshape(1, -1).astype(F32)
    vspec = lambda n: pl.BlockSpec((1, n), lambda b, h, i: (0, 0))
    return pl.pallas_call(
        functools.partial(_diff_kernel, n_kv=n_kv, tk=tk, tq=tq, lambda_init=lambda_init),
        grid=(batch, DIFF_HEADS, nq),
        in_specs=[
            vspec(DIFF_QK_DIM), vspec(DIFF_QK_DIM), vspec(DIFF_QK_DIM), vspec(DIFF_QK_DIM),
            vspec(DIFF_V_DIM),
            pl.BlockSpec((2, rows, DIFF_QK_DIM), lambda b, h, i: (h, b * nq + i, 0)),
            pl.BlockSpec((2, seq, DIFF_QK_DIM), lambda b, h, i: (DIFF_HEADS + h, b, 0)),
            pl.BlockSpec((None, n_kv, DIFF_V_DIM, tk), lambda b, h, i: (h, b, 0, 0)),
        ],
        out_specs=pl.BlockSpec((rows, DIFF_V_DIM), lambda b, h, i: (b * nq + i, h)),
        out_shape=jax.ShapeDtypeStruct((t, DV_COLS), BF16),
        scratch_shapes=[pltpu.VMEM((2, DIFF_V_DIM, tq), F32),
                        pltpu.VMEM((2, tk, tq), F32),
                        pltpu.VMEM((2, tk, tq), F32),
                        pltpu.VMEM((2, 1, tq), F32),
                        pltpu.VMEM((2, 1, tq), F32)],
        compiler_params=_params(("parallel", "parallel", "arbitrary")),
        name="diff_attention",
    )(vec(lq1), vec(lk1), vec(lq2), vec(lk2), vec(subln_w), dqk, dqk, dvt)


def _outproj_kernel(ro_ref, do_ref, w_ref, x_ref, g_ref, b_ref, o32_ref, o16_ref,
                    mix_scr, pre_scr, *, ni, nj):
    i = pl.program_id(0)
    j = pl.program_id(1)
    half = ro_ref.shape[1]
    rows = o32_ref.shape[0]

    @pl.when(jnp.logical_and(j == 0, i < ni))
    def _():
        mix_scr[:, :half] = ro_ref[...]
        mix_scr[:, half:] = do_ref[...]

    def produce():
        pre_scr[i % 2, j] = DEEPNORM_ALPHA * x_ref[...] + jnp.dot(
            mix_scr[...], w_ref[...], preferred_element_type=F32)

    def normalise():
        slot = (i + 1) % 2
        rs = pl.ds(pl.multiple_of(j * rows, rows), rows)
        pre = jnp.concatenate([pre_scr[slot, c, rs, :] for c in range(nj)], axis=1)
        y = _layer_norm(pre, g_ref[...], b_ref[...])
        o32_ref[...] = y
        o16_ref[...] = y.astype(o16_ref.dtype)

    def both():
        normalise()
        produce()

    _three_phase(i, ni, produce, both, normalise)


def _out_projection(ro, do, w_out, x, ln_g, ln_b):
    t = x.shape[0]
    tm = _tile(t, 512)
    nj, _, tn = w_out.shape
    ni = t // tm
    rows = tm // nj
    half = RV_COLS
    tile = lambda i: jnp.minimum(i, ni - 1)
    prev = lambda i, j: (jnp.where(i == 0, 0, (i - 1) * nj + j), 0)
    return pl.pallas_call(
        functools.partial(_outproj_kernel, ni=ni, nj=nj),
        grid=(ni + 1, nj),
        in_specs=[
            pl.BlockSpec((tm, half), lambda i, j: (tile(i), 0)),
            pl.BlockSpec((tm, half), lambda i, j: (tile(i), 0)),
            pl.BlockSpec((None, 2 * half, tn), lambda i, j: (j, 0, 0)),
            pl.BlockSpec((tm, tn), lambda i, j: (tile(i), j)),
            pl.BlockSpec((1, D_MODEL), lambda i, j: (0, 0)),
            pl.BlockSpec((1, D_MODEL), lambda i, j: (0, 0)),
        ],
        out_specs=[pl.BlockSpec((rows, D_MODEL), prev), pl.BlockSpec((rows, D_MODEL), prev)],
        out_shape=[jax.ShapeDtypeStruct((t, D_MODEL), F32),
                   jax.ShapeDtypeStruct((t, D_MODEL), BF16)],
        scratch_shapes=[pltpu.VMEM((tm, 2 * half), BF16),
                        pltpu.VMEM((2, nj, tm, tn), F32)],
        compiler_params=_params(("arbitrary", "arbitrary")),
        name="out_projection",
    )(ro, do, w_out, x, ln_g.reshape(1, -1), ln_b.reshape(1, -1))


def _gate_up_kernel(x_ref, wg_ref, wu_ref, o_ref):
    x = x_ref[...]
    g = jnp.dot(x, wg_ref[...], preferred_element_type=F32)
    u = jnp.dot(x, wu_ref[...], preferred_element_type=F32)
    o_ref[...] = (_swish(g) * u).astype(o_ref.dtype)


def _gate_up(xb, wg, wu):
    t, d = xb.shape
    n_tiles, _, tn = wg.shape
    n = n_tiles * tn
    tm = _tile(t, 1024)
    return pl.pallas_call(
        _gate_up_kernel,
        grid=(t // tm, n_tiles),
        in_specs=[pl.BlockSpec((tm, d), lambda i, j: (i, 0)),
                  pl.BlockSpec((None, d, tn), lambda i, j: (j, 0, 0)),
                  pl.BlockSpec((None, d, tn), lambda i, j: (j, 0, 0))],
        out_specs=pl.BlockSpec((tm, tn), lambda i, j: (i, j)),
        out_shape=jax.ShapeDtypeStruct((t, n), BF16),
        compiler_params=_params(("parallel", "arbitrary")),
        name="ffn_gate_up",
    )(xb, wg, wu)


def _down_kernel(h_ref, w_ref, x_ref, g_ref, b_ref, o_ref, acc_scr, *, ni, n_ln):
    i = pl.program_id(0)
    k = pl.program_id(1)
    rows = o_ref.shape[0]
    width = o_ref.shape[1]

    def produce():
        h = h_ref[...]
        slot = i % 2
        for c in range(width // DOWN_COL_CHUNK):
            sl = slice(c * DOWN_COL_CHUNK, (c + 1) * DOWN_COL_CHUNK)
            part = jnp.dot(h, w_ref[:, sl], preferred_element_type=F32)
            acc_scr[slot, :, sl] = jnp.where(k > 0, acc_scr[slot, :, sl], 0.0) + part

    def normalise():
        slot = (i + 1) % 2
        chunk = jnp.minimum(k, n_ln - 1)
        rs = pl.ds(pl.multiple_of(chunk * rows, rows), rows)
        pre = DEEPNORM_ALPHA * x_ref[...] + acc_scr[slot, rs, :]
        o_ref[...] = _layer_norm(pre, g_ref[...], b_ref[...])

    def first():
        @pl.when(k == 0)
        def _():
            acc_scr[...] = jnp.zeros_like(acc_scr)

        produce()

    def both():
        normalise()
        produce()

    _three_phase(i, ni, first, both, normalise)


def _down_projection(h, wd, x1, ln_g, ln_b):
    t, kdim = h.shape
    tm = _tile(t, 512)
    tk = 1024
    ni = t // tm
    nk = kdim // tk
    rows = LN_ROW_CHUNK
    n_ln = tm // rows
    assert n_ln <= nk, (n_ln, nk)
    tile = lambda i: jnp.minimum(i, ni - 1)
    prev = lambda i, k: (jnp.where(i == 0, 0, (i - 1) * n_ln + jnp.minimum(k, n_ln - 1)), 0)
    return pl.pallas_call(
        functools.partial(_down_kernel, ni=ni, n_ln=n_ln),
        grid=(ni + 1, nk),
        in_specs=[pl.BlockSpec((tm, tk), lambda i, k: (tile(i), k)),
                  pl.BlockSpec((tk, D_MODEL), lambda i, k: (k, 0)),
                  pl.BlockSpec((rows, D_MODEL), prev),
                  pl.BlockSpec((1, D_MODEL), lambda i, k: (0, 0)),
                  pl.BlockSpec((1, D_MODEL), lambda i, k: (0, 0))],
        out_specs=pl.BlockSpec((rows, D_MODEL), prev),
        out_shape=jax.ShapeDtypeStruct((t, D_MODEL), F32),
        scratch_shapes=[pltpu.VMEM((2, tm, D_MODEL), F32)],
        compiler_params=_params(("arbitrary", "arbitrary")),
        name="ffn_down",
    )(h, wd, x1, ln_g.reshape(1, -1), ln_b.reshape(1, -1))


def _rope_tables(seq):
    pos = jnp.arange(seq, dtype=F32)

    def cos_sin(rot_dim, base):
        inv_freq = base ** (-jnp.arange(0, rot_dim, 2, dtype=F32) / rot_dim)
        ang = pos[:, None] * inv_freq[None, :]
        return jnp.cos(ang), jnp.sin(ang)

    c, s = cos_sin(RET_QK_DIM, RET_ROT_BASE)
    ret_tab = jnp.concatenate([c, c, -s, s], axis=-1)

    c, s = cos_sin(DIFF_ROT_DIM, ROPE_THETA)
    half = DIFF_ROT_DIM // 2
    rest = DIFF_QK_DIM - DIFF_ROT_DIM
    ones = jnp.ones((seq, rest), F32)
    zeros = jnp.zeros((seq, rest), F32)
    zh = jnp.zeros((seq, half), F32)
    diff_tab = jnp.concatenate([c, c, ones, zh, s, zeros, -s, zh, zeros], axis=-1)
    return ret_tab, diff_tab


def _col_tiles(w):
    k, n = w.shape
    return w.reshape(k, n // WEIGHT_COLS, WEIGHT_COLS).transpose(1, 0, 2)


def _prepare_weights(w_in, w_out, w_gate, w_up, w_down):
    wb = w_in.astype(BF16)
    c0 = 2 * RQ_COLS
    c1 = c0 + 2 * RV_COLS
    c2 = c1 + 2 * DQ_COLS
    pad = D_FF_PAD - D_FF
    zc = jnp.zeros((w_gate.shape[0], pad), BF16)
    wg = jnp.concatenate([w_gate.astype(BF16), zc], axis=1)
    wu = jnp.concatenate([w_up.astype(BF16), zc], axis=1)
    wd = jnp.concatenate([w_down.astype(BF16), jnp.zeros((pad, w_down.shape[1]), BF16)], axis=0)
    return (_col_tiles(wb[:, :c0]), _col_tiles(wb[:, c0:c1]), _col_tiles(wb[:, c1:c2]),
            _col_tiles(wb[:, c2:]), _col_tiles(w_out.astype(BF16)), _col_tiles(wg),
            _col_tiles(wu), wd)


def _encoder_layer(x, weights, tables, dec_f, dec_b, gn_w, lq1, lk1, lq2, lk2, subln_w,
                   ln1_g, ln1_b, ln2_g, ln2_b, lambda_init):
    batch, seq, d = x.shape
    w_rqk, w_rvg, w_dqk, w_dv, w_out, wg, wu, wd = weights
    x2 = x.reshape(batch * seq, d)
    ret_tab, diff_tab = tables

    rvg, xb = _project(x2, w_rvg, seq, "plain", RET_V_DIM, cast=True)
    rqk = _project(xb, w_rqk, seq, "rope_full", RET_QK_DIM, ret_tab,
                   (1.0, RET_QK_DIM ** -0.5), RQ_COLS)
    dqk = _project(xb, w_dqk, seq, "rope_partial", DIFF_QK_DIM, diff_tab,
                   (DIFF_QK_DIM ** -0.5 * LOG2_E, 1.0), DQ_COLS)
    dvt = _project(xb, w_dv, seq, "plain_t", DIFF_V_DIM)

    ro = _retention(rqk, rvg, dec_f, dec_b, gn_w, batch, seq)
    do = _diff_attention(dqk, dvt, lq1, lk1, lq2, lk2, subln_w, batch, seq, lambda_init)

    x1, x1b = _out_projection(ro, do, w_out, x2, ln1_g, ln1_b)
    h = _gate_up(x1b, wg, wu)
    y = _down_projection(h, wd, x1, ln2_g, ln2_b)
    return y.reshape(batch, seq, d)


def kernel(x_prompt, x_sample, w_in, ret_decay_f, ret_decay_b, ret_gn_w, diff_lambda_q1,
           diff_lambda_k1, diff_lambda_q2, diff_lambda_k2, diff_subln_w, w_out, ln1_g, ln1_b,
           w_gate, w_up, w_down, ln2_g, ln2_b):
    y_prompt, y_sample = x_prompt, x_sample
    for l in range(DEPTH):
        lambda_init = 0.8 - 0.6 * math.exp(-0.3 * l)
        weights = _prepare_weights(w_in[l], w_out[l], w_gate[l], w_up[l], w_down[l])
        tables = _rope_tables(max(y_prompt.shape[1], y_sample.shape[1]))
        rest = (tables, ret_decay_f[l], ret_decay_b[l], ret_gn_w[l], diff_lambda_q1[l], diff_lambda_k1[l],
                diff_lambda_q2[l], diff_lambda_k2[l], diff_subln_w[l], ln1_g[l], ln1_b[l],
                ln2_g[l], ln2_b[l], lambda_init)
        y_prompt = _encoder_layer(y_prompt, weights, *rest)
        y_sample = _encoder_layer(y_sample, weights, *rest)
    return (y_prompt, y_sample)
```

```python
import functools
import math

import jax
import jax.numpy as jnp
from jax import lax
from jax.experimental import pallas as pl
from jax.experimental.pallas import tpu as pltpu

D_MODEL = 4096
DEPTH = 1
RET_HEADS = 8
RET_QK_DIM = 128
RET_V_DIM = 256
RET_ROT_BASE = 10000.0
DIFF_HEADS = 8
DIFF_QK_DIM = 128
DIFF_V_DIM = 256
DIFF_ROT_DIM = DIFF_QK_DIM // 4
ROPE_THETA = 500000.0
D_FF = 11008
DEEPNORM_ALPHA = (2.0 * DEPTH) ** 0.25
LN_EPS = 1e-5
GN_EPS = 1e-6
RMS_EPS = 1e-5

RQ_COLS = RET_HEADS * RET_QK_DIM
RV_COLS = RET_HEADS * RET_V_DIM
DQ_COLS = DIFF_HEADS * 2 * DIFF_QK_DIM
DV_COLS = DIFF_HEADS * DIFF_V_DIM

LANES = 128
VMEM_LIMIT_BYTES = 56 * 1024 * 1024
D_FF_PAD = 11264
OUT_COL_GROUPS = 4
PROJ_ROWS = 1024
RET_CHUNK = 256
RET_BLOCK = 1024
ATTN_Q_ROWS = 512
ATTN_Q_BLOCKS = 4
DOWN_COL_CHUNK = 1024
LN_ROW_CHUNK = 64
LOG2_E = math.log2(math.e)

BF16 = jnp.bfloat16
F32 = jnp.float32

NT_DIMS = (((1,), (1,)), ((), ()))
TN_DIMS = (((0,), (0,)), ((), ()))


def _tile(n, pref):
    t = pref
    while t > 8 and n % t:
        t //= 2
    assert n % t == 0, (n, pref)
    return t


def _params(sem):
    return pltpu.CompilerParams(dimension_semantics=sem, vmem_limit_bytes=VMEM_LIMIT_BYTES)


def _swish(g):
    return g * (1.0 / (1.0 + jnp.exp(-g)))


def _layer_norm(pre, g, b):
    mu = jnp.mean(pre, axis=-1, keepdims=True)
    var = jnp.mean(jnp.square(pre - mu), axis=-1, keepdims=True)
    return (pre - mu) * lax.rsqrt(var + LN_EPS) * g + b


def _three_phase(i, n, first, middle, last):
    pl.when(i == 0)(first)
    pl.when(jnp.logical_and(i > 0, i < n))(middle)
    pl.when(i == n)(last)


def _proj_kernel(x_ref, w_ref, *rest, mode, hw, scales, split):
    if mode in ("rope_full", "rope_partial"):
        tab_ref, o_ref = rest
        scale = jnp.where(pl.program_id(1) < split, scales[0], scales[1]).astype(F32)
    else:
        (o_ref,) = rest
    tm = x_ref.shape[0]
    n_split = 1 if mode == "plain" else 2
    rows = tm // n_split
    for r in range(n_split):
        rs = slice(r * rows, (r + 1) * rows)
        acc = jnp.dot(x_ref[rs, :], w_ref[...], preferred_element_type=F32)
        for c in range(o_ref.shape[0]):
            blk = acc[:, c * hw:(c + 1) * hw]
            if mode == "rope_full":
                cos = tab_ref[rs, 0:LANES]
                sin = tab_ref[rs, LANES:2 * LANES]
                blk = (blk * cos + pltpu.roll(blk, LANES // 2, 1) * sin) * scale
            elif mode == "rope_partial":
                half = DIFF_ROT_DIM // 2
                a = tab_ref[rs, 0:LANES]
                b = tab_ref[rs, LANES:2 * LANES]
                c2 = tab_ref[rs, 2 * LANES:3 * LANES]
                blk = (blk * a + pltpu.roll(blk, half, 1) * b
                       + pltpu.roll(blk, LANES - half, 1) * c2) * scale
            if mode == "plain_t":
                o_ref[c, :, rs] = blk.T.astype(o_ref.dtype)
            else:
                o_ref[c, rs, :] = blk.astype(o_ref.dtype)


def _project(x, w, seq, mode, hw, tab=None, scales=None, split_cols=None):
    t, d = x.shape
    n = w.shape[1]
    tm = _tile(seq, PROJ_ROWS)
    tn = _tile(n, 512)
    grid = (t // tm, n // tn)
    in_specs = [pl.BlockSpec((tm, d), lambda i, j: (i, 0)),
                pl.BlockSpec((d, tn), lambda i, j: (0, j))]
    args = [x, w]
    if tab is not None:
        per_seq = seq // tm
        in_specs.append(pl.BlockSpec((tm, tab.shape[1]), lambda i, j: (i % per_seq, 0)))
        args.append(tab)
    if mode == "plain_t":
        out_spec = pl.BlockSpec((tn // hw, None, hw, tm), lambda i, j: (j, i, 0, 0))
        out_shape = jax.ShapeDtypeStruct((n // hw, t // tm, hw, tm), BF16)
    else:
        out_spec = pl.BlockSpec((tn // hw, tm, hw), lambda i, j: (j, i, 0))
        out_shape = jax.ShapeDtypeStruct((n // hw, t, hw), BF16)
    return pl.pallas_call(
        functools.partial(_proj_kernel, mode=mode, hw=hw, scales=scales,
                          split=None if split_cols is None else split_cols // tn),
        grid=grid,
        in_specs=in_specs,
        out_specs=out_spec,
        out_shape=out_shape,
        compiler_params=_params(("parallel", "arbitrary")),
        name="proj_" + mode,
    )(*args)


def _ret_kernel(decf_ref, decb_ref, q_ref, k_ref, v_ref, g_ref, gnw_ref, o_ref,
                sf_scr, sb_scr, sball_scr, *, nblk, nb, chunk):
    s = pl.program_id(2)
    lg_f = -jnp.exp(decf_ref[...])
    lg_b = -jnp.exp(decb_ref[...])
    row = lax.broadcasted_iota(jnp.int32, (chunk, 1), 0).astype(F32)

    @pl.when(s == 0)
    def _():
        sb_scr[...] = jnp.zeros_like(sb_scr)

    @pl.when(s < nblk)
    def _():
        blk = nblk - 1 - s
        zeta_b = jnp.exp(row * lg_b)
        decay_b = jnp.exp(chunk * lg_b)
        state = sb_scr[...]
        for ci in reversed(range(nb)):
            rs = slice(ci * chunk, (ci + 1) * chunk)
            sball_scr[blk * nb + ci] = state.astype(BF16)
            kz = (k_ref[rs, :].astype(F32) * zeta_b).astype(BF16)
            state = decay_b * state + lax.dot_general(kz, v_ref[rs, :], TN_DIMS,
                                                      preferred_element_type=F32)
        sb_scr[...] = state

    @pl.when(s == nblk)
    def _():
        sf_scr[...] = jnp.zeros_like(sf_scr)

    @pl.when(s >= nblk)
    def _():
        blk = s - nblk
        rel = (lax.broadcasted_iota(jnp.int32, (chunk, chunk), 0)
               - lax.broadcasted_iota(jnp.int32, (chunk, chunk), 1)).astype(F32)
        dmat = (jnp.where(rel >= 0, jnp.exp(jnp.maximum(rel, 0.0) * lg_f), 0.0)
                + jnp.where(rel <= 0, jnp.exp(jnp.maximum(-rel, 0.0) * lg_b), 0.0))
        xi_f = jnp.exp((row + 1.0) * lg_f)
        xi_b = jnp.exp((chunk - row) * lg_b)
        zeta_f = jnp.exp((chunk - 1.0 - row) * lg_f)
        decay_f = jnp.exp(chunk * lg_f)
        gnw = gnw_ref[...]
        state = sf_scr[...]
        for ci in range(nb):
            rs = slice(ci * chunk, (ci + 1) * chunk)
            q = q_ref[rs, :]
            k = k_ref[rs, :]
            v = v_ref[rs, :]
            sc = lax.dot_general(q, k, NT_DIMS, preferred_element_type=F32)
            qf = q.astype(F32)
            o = jnp.dot((sc * dmat).astype(BF16), v, preferred_element_type=F32)
            o += jnp.dot((qf * xi_f).astype(BF16), state.astype(BF16), preferred_element_type=F32)
            o += jnp.dot((qf * xi_b).astype(BF16), sball_scr[blk * nb + ci],
                         preferred_element_type=F32)
            kz = (k.astype(F32) * zeta_f).astype(BF16)
            state = decay_f * state + lax.dot_general(kz, v, TN_DIMS, preferred_element_type=F32)
            mu = jnp.mean(o, axis=-1, keepdims=True)
            var = jnp.mean(jnp.square(o - mu), axis=-1, keepdims=True)
            on = (o - mu) * lax.rsqrt(var + GN_EPS) * gnw
            o_ref[rs, :] = (_swish(g_ref[rs, :].astype(F32)) * on).astype(o_ref.dtype)
        sf_scr[...] = state


def _retention(rqk, rvg, dec_f, dec_b, gn_w, batch, seq):
    t = rqk.shape[1]
    rows = _tile(seq, RET_BLOCK)
    chunk = _tile(rows, RET_CHUNK)
    nb = rows // chunk
    nblk = seq // rows

    def bidx(b, s):
        return b * nblk + jnp.where(s < nblk, nblk - 1 - s, s - nblk)

    def oidx(b, s):
        return b * nblk + jnp.maximum(s - nblk, 0)

    return pl.pallas_call(
        functools.partial(_ret_kernel, nblk=nblk, nb=nb, chunk=chunk),
        grid=(batch, RET_HEADS, 2 * nblk),
        in_specs=[
            pl.BlockSpec((None, 1, 1), lambda b, h, s: (h, 0, 0)),
            pl.BlockSpec((None, 1, 1), lambda b, h, s: (h, 0, 0)),
            pl.BlockSpec((None, rows, RET_QK_DIM), lambda b, h, s: (h, oidx(b, s), 0)),
            pl.BlockSpec((None, rows, RET_QK_DIM), lambda b, h, s: (RET_HEADS + h, bidx(b, s), 0)),
            pl.BlockSpec((None, rows, RET_V_DIM), lambda b, h, s: (h, bidx(b, s), 0)),
            pl.BlockSpec((None, rows, RET_V_DIM), lambda b, h, s: (RET_HEADS + h, oidx(b, s), 0)),
            pl.BlockSpec((None, 1, RET_V_DIM), lambda b, h, s: (h, 0, 0)),
        ],
        out_specs=pl.BlockSpec((rows, RET_V_DIM), lambda b, h, s: (oidx(b, s), h)),
        out_shape=jax.ShapeDtypeStruct((t, RV_COLS), BF16),
        scratch_shapes=[
            pltpu.VMEM((RET_QK_DIM, RET_V_DIM), F32),
            pltpu.VMEM((RET_QK_DIM, RET_V_DIM), F32),
            pltpu.VMEM((nblk * nb, RET_QK_DIM, RET_V_DIM), BF16),
        ],
        compiler_params=_params(("parallel", "parallel", "arbitrary")),
        name="retention",
    )(dec_f.reshape(RET_HEADS, 1, 1), dec_b.reshape(RET_HEADS, 1, 1),
      rqk, rqk, rvg, rvg, gn_w.reshape(RET_HEADS, 1, RET_V_DIM))


def _diff_kernel(lq1_ref, lk1_ref, lq2_ref, lk2_ref, sub_ref, q_ref, k_ref, vt_ref, o_ref,
                 acc_scr, s0_scr, s1_scr, m0_scr, m1_scr, *, n_kv, tk, tq, lambda_init):
    n_blocks = q_ref.shape[1] // tq
    s0 = (s0_scr, m0_scr)
    s1 = (s1_scr, m1_scr)
    bufs = (s0, s1)
    lam = (jnp.exp(jnp.sum(lq1_ref[...] * lk1_ref[...], axis=1, keepdims=True))
           - jnp.exp(jnp.sum(lq2_ref[...] * lk2_ref[...], axis=1, keepdims=True)) + lambda_init)

    def scores(r, j, buf):
        s_scr, m_scr = buf
        start = pl.multiple_of(j * tk, tk)
        for a in range(2):
            st = lax.dot_general(k_ref[a, pl.ds(start, tk), :],
                                 q_ref[a, r * tq:(r + 1) * tq, :], NT_DIMS,
                                 preferred_element_type=F32)
            s_scr[a] = st
            m_scr[a] = jnp.max(st, axis=0, keepdims=True)

    def absorb(j, buf, carry):
        s_scr, m_scr = buf
        vt = vt_ref[j]
        out = []
        for a in range(2):
            m_prev, l_prev = carry[a]
            st = s_scr[a]
            m_new = jnp.maximum(m_prev, m_scr[a])
            alpha = jnp.exp2(m_prev - m_new)
            p = jnp.exp2(st - m_new)
            l_new = alpha * l_prev + jnp.sum(p, axis=0, keepdims=True)
            acc_scr[a] = alpha * acc_scr[a] + jnp.dot(vt, p.astype(BF16),
                                                      preferred_element_type=F32)
            out.append((m_new, l_new))
        return tuple(out)

    def finalize(r, carry):
        (_, l0), (_, l1) = carry
        ot = acc_scr[0] * (1.0 / l0) - lam * (acc_scr[1] * (1.0 / l1))
        o = ot.T
        o = o * lax.rsqrt(jnp.mean(jnp.square(o), axis=-1, keepdims=True) + RMS_EPS)
        o_ref[r * tq:(r + 1) * tq, :] = (o * sub_ref[...] * (1.0 - lambda_init)).astype(o_ref.dtype)

    init = (jnp.full((1, tq), -jnp.inf, F32), jnp.zeros((1, tq), F32))
    scores(0, 0, s0)
    for r in range(n_blocks):
        acc_scr[...] = jnp.zeros_like(acc_scr)
        carry = (init, init)
        more = r + 1 < n_blocks
        if n_kv > 1:
            def pair(i, carry, r=r):
                scores(r, 2 * i + 1, s1)
                carry = absorb(2 * i, s0, carry)
                scores(r, 2 * i + 2, s0)
                return absorb(2 * i + 1, s1, carry)

            carry = lax.fori_loop(0, n_kv // 2 - 1, pair, carry)
            scores(r, n_kv - 1, s1)
            carry = absorb(n_kv - 2, s0, carry)
            if more:
                scores(r + 1, 0, s0)
            carry = absorb(n_kv - 1, s1, carry)
        else:
            if more:
                scores(r + 1, 0, bufs[(r + 1) % 2])
            carry = absorb(0, bufs[r % 2], carry)
        finalize(r, carry)


def _diff_attention(dqk, dvt, lq1, lk1, lq2, lk2, subln_w, batch, seq, lambda_init):
    t = dqk.shape[1]
    tk = dvt.shape[3]
    n_kv = seq // tk
    assert n_kv == 1 or n_kv % 2 == 0, n_kv
    tq = _tile(seq, ATTN_Q_ROWS)
    rows = _tile(seq, ATTN_Q_ROWS * ATTN_Q_BLOCKS)
    nq = seq // rows
    vec = lambda a: a.reshape(1, -1).astype(F32)
    vspec = lambda n: pl.BlockSpec((1, n), lambda b, h, i: (0, 0))
    return pl.pallas_call(
        functools.partial(_diff_kernel, n_kv=n_kv, tk=tk, tq=tq, lambda_init=lambda_init),
        grid=(batch, DIFF_HEADS, nq),
        in_specs=[
            vspec(DIFF_QK_DIM), vspec(DIFF_QK_DIM), vspec(DIFF_QK_DIM), vspec(DIFF_QK_DIM),
            vspec(DIFF_V_DIM),
            pl.BlockSpec((2, rows, DIFF_QK_DIM), lambda b, h, i: (h, b * nq + i, 0)),
            pl.BlockSpec((2, seq, DIFF_QK_DIM), lambda b, h, i: (DIFF_HEADS + h, b, 0)),
            pl.BlockSpec((None, n_kv, DIFF_V_DIM, tk), lambda b, h, i: (h, b, 0, 0)),
        ],
        out_specs=pl.BlockSpec((rows, DIFF_V_DIM), lambda b, h, i: (b * nq + i, h)),
        out_shape=jax.ShapeDtypeStruct((t, DV_COLS), BF16),
        scratch_shapes=[pltpu.VMEM((2, DIFF_V_DIM, tq), F32),
                        pltpu.VMEM((2, tk, tq), F32),
                        pltpu.VMEM((2, tk, tq), F32),
                        pltpu.VMEM((2, 1, tq), F32),
                        pltpu.VMEM((2, 1, tq), F32)],
        compiler_params=_params(("parallel", "parallel", "arbitrary")),
        name="diff_attention",
    )(vec(lq1), vec(lk1), vec(lq2), vec(lk2), vec(subln_w), dqk, dqk, dvt)


def _outproj_kernel(ro_ref, do_ref, w_ref, x_ref, g_ref, b_ref, o32_ref, o16_ref,
                    mix_scr, pre_scr, *, ni, nj):
    i = pl.program_id(0)
    j = pl.program_id(1)
    half = ro_ref.shape[1]
    rows = o32_ref.shape[0]

    @pl.when(jnp.logical_and(j == 0, i < ni))
    def _():
        mix_scr[:, :half] = ro_ref[...]
        mix_scr[:, half:] = do_ref[...]

    def produce():
        pre_scr[i % 2, j] = jnp.dot(mix_scr[...], w_ref[j], preferred_element_type=F32)

    def normalise():
        slot = (i + 1) % 2
        rs = pl.ds(pl.multiple_of(j * rows, rows), rows)
        pre = jnp.concatenate([pre_scr[slot, c, rs, :] for c in range(nj)], axis=1)
        pre = pre + DEEPNORM_ALPHA * x_ref[...]
        y = _layer_norm(pre, g_ref[...], b_ref[...])
        o32_ref[...] = y
        o16_ref[...] = y.astype(o16_ref.dtype)

    def both():
        normalise()
        produce()

    _three_phase(i, ni, produce, both, normalise)


def _out_projection(ro, do, w_out, x, ln_g, ln_b):
    t = x.shape[0]
    tm = _tile(t, 256)
    nj, _, tn = w_out.shape
    ni = t // tm
    rows = tm // nj
    half = RV_COLS
    tile = lambda i: jnp.minimum(i, ni - 1)
    prev = lambda i, j: (jnp.where(i == 0, 0, (i - 1) * nj + j), 0)
    return pl.pallas_call(
        functools.partial(_outproj_kernel, ni=ni, nj=nj),
        grid=(ni + 1, nj),
        in_specs=[
            pl.BlockSpec((tm, half), lambda i, j: (tile(i), 0)),
            pl.BlockSpec((tm, half), lambda i, j: (tile(i), 0)),
            pl.BlockSpec((nj, 2 * half, tn), lambda i, j: (0, 0, 0)),
            pl.BlockSpec((rows, D_MODEL), prev),
            pl.BlockSpec((1, D_MODEL), lambda i, j: (0, 0)),
            pl.BlockSpec((1, D_MODEL), lambda i, j: (0, 0)),
        ],
        out_specs=[pl.BlockSpec((rows, D_MODEL), prev), pl.BlockSpec((rows, D_MODEL), prev)],
        out_shape=[jax.ShapeDtypeStruct((t, D_MODEL), F32),
                   jax.ShapeDtypeStruct((t, D_MODEL), BF16)],
        scratch_shapes=[pltpu.VMEM((tm, 2 * half), BF16),
                        pltpu.VMEM((2, nj, tm, tn), F32)],
        compiler_params=_params(("arbitrary", "arbitrary")),
        name="out_projection",
    )(ro, do, w_out, x, ln_g.reshape(1, -1), ln_b.reshape(1, -1))


def _gate_up_kernel(x_ref, wg_ref, wu_ref, o_ref):
    x = x_ref[...]
    g = jnp.dot(x, wg_ref[...], preferred_element_type=F32)
    u = jnp.dot(x, wu_ref[...], preferred_element_type=F32)
    o_ref[...] = (_swish(g) * u).astype(o_ref.dtype)


def _gate_up(xb, wg, wu):
    t, d = xb.shape
    n = wg.shape[1]
    tm = _tile(t, 1024)
    tn = 512
    return pl.pallas_call(
        _gate_up_kernel,
        grid=(t // tm, n // tn),
        in_specs=[pl.BlockSpec((tm, d), lambda i, j: (i, 0)),
                  pl.BlockSpec((d, tn), lambda i, j: (0, j)),
                  pl.BlockSpec((d, tn), lambda i, j: (0, j))],
        out_specs=pl.BlockSpec((tm, tn), lambda i, j: (i, j)),
        out_shape=jax.ShapeDtypeStruct((t, n), BF16),
        compiler_params=_params(("parallel", "arbitrary")),
        name="ffn_gate_up",
    )(xb, wg, wu)


def _down_kernel(h_ref, w_ref, x_ref, g_ref, b_ref, o_ref, acc_scr, *, ni, n_ln):
    i = pl.program_id(0)
    k = pl.program_id(1)
    rows = o_ref.shape[0]
    width = o_ref.shape[1]

    def produce():
        h = h_ref[...]
        slot = i % 2
        for c in range(width // DOWN_COL_CHUNK):
            sl = slice(c * DOWN_COL_CHUNK, (c + 1) * DOWN_COL_CHUNK)
            part = jnp.dot(h, w_ref[:, sl], preferred_element_type=F32)
            acc_scr[slot, :, sl] = jnp.where(k > 0, acc_scr[slot, :, sl], 0.0) + part

    def normalise():
        slot = (i + 1) % 2
        chunk = jnp.minimum(k, n_ln - 1)
        rs = pl.ds(pl.multiple_of(chunk * rows, rows), rows)
        pre = DEEPNORM_ALPHA * x_ref[...] + acc_scr[slot, rs, :]
        o_ref[...] = _layer_norm(pre, g_ref[...], b_ref[...])

    def first():
        @pl.when(k == 0)
        def _():
            acc_scr[...] = jnp.zeros_like(acc_scr)

        produce()

    def both():
        normalise()
        produce()

    _three_phase(i, ni, first, both, normalise)


def _down_projection(h, wd, x1, ln_g, ln_b):
    t, kdim = h.shape
    tm = _tile(t, 1024)
    tk = 512
    ni = t // tm
    nk = kdim // tk
    rows = LN_ROW_CHUNK
    n_ln = tm // rows
    assert n_ln <= nk, (n_ln, nk)
    tile = lambda i: jnp.minimum(i, ni - 1)
    prev = lambda i, k: (jnp.where(i == 0, 0, (i - 1) * n_ln + jnp.minimum(k, n_ln - 1)), 0)
    return pl.pallas_call(
        functools.partial(_down_kernel, ni=ni, n_ln=n_ln),
        grid=(ni + 1, nk),
        in_specs=[pl.BlockSpec((tm, tk), lambda i, k: (tile(i), k)),
                  pl.BlockSpec((tk, D_MODEL), lambda i, k: (k, 0)),
                  pl.BlockSpec((rows, D_MODEL), prev),
                  pl.BlockSpec((1, D_MODEL), lambda i, k: (0, 0)),
                  pl.BlockSpec((1, D_MODEL), lambda i, k: (0, 0))],
        out_specs=pl.BlockSpec((rows, D_MODEL), prev),
        out_shape=jax.ShapeDtypeStruct((t, D_MODEL), F32),
        scratch_shapes=[pltpu.VMEM((2, tm, D_MODEL), F32)],
        compiler_params=_params(("arbitrary", "arbitrary")),
        name="ffn_down",
    )(h, wd, x1, ln_g.reshape(1, -1), ln_b.reshape(1, -1))


def _rope_tables(seq):
    pos = jnp.arange(seq, dtype=F32)

    def cos_sin(rot_dim, base):
        inv_freq = base ** (-jnp.arange(0, rot_dim, 2, dtype=F32) / rot_dim)
        ang = pos[:, None] * inv_freq[None, :]
        return jnp.cos(ang), jnp.sin(ang)

    c, s = cos_sin(RET_QK_DIM, RET_ROT_BASE)
    ret_tab = jnp.concatenate([c, c, -s, s], axis=-1)

    c, s = cos_sin(DIFF_ROT_DIM, ROPE_THETA)
    half = DIFF_ROT_DIM // 2
    rest = DIFF_QK_DIM - DIFF_ROT_DIM
    ones = jnp.ones((seq, rest), F32)
    zeros = jnp.zeros((seq, rest), F32)
    zh = jnp.zeros((seq, half), F32)
    diff_tab = jnp.concatenate([c, c, ones, zh, s, zeros, -s, zh, zeros], axis=-1)
    return ret_tab, diff_tab


def _prepare_weights(w_in, w_out, w_gate, w_up, w_down):
    wb = w_in.astype(BF16)
    c0 = 2 * RQ_COLS
    c1 = c0 + 2 * RV_COLS
    c2 = c1 + 2 * DQ_COLS
    pad = D_FF_PAD - D_FF
    zc = jnp.zeros((w_gate.shape[0], pad), BF16)
    wg = jnp.concatenate([w_gate.astype(BF16), zc], axis=1)
    wu = jnp.concatenate([w_up.astype(BF16), zc], axis=1)
    wd = jnp.concatenate([w_down.astype(BF16), jnp.zeros((pad, w_down.shape[1]), BF16)], axis=0)
    k, n = w_out.shape
    wo = w_out.astype(BF16).reshape(k, OUT_COL_GROUPS, n // OUT_COL_GROUPS).transpose(1, 0, 2)
    return wb[:, :c0], wb[:, c0:c1], wb[:, c1:c2], wb[:, c2:], wo, wg, wu, wd


def _encoder_layer(x, weights, tables, dec_f, dec_b, gn_w, lq1, lk1, lq2, lk2, subln_w,
                   ln1_g, ln1_b, ln2_g, ln2_b, lambda_init):
    batch, seq, d = x.shape
    w_rqk, w_rvg, w_dqk, w_dv, w_out, wg, wu, wd = weights
    x2 = x.reshape(batch * seq, d)
    ret_tab, diff_tab = tables

    xb = x2.astype(BF16)
    rvg = _project(xb, w_rvg, seq, "plain", RET_V_DIM)
    rqk = _project(xb, w_rqk, seq, "rope_full", RET_QK_DIM, ret_tab,
                   (1.0, RET_QK_DIM ** -0.5), RQ_COLS)
    dqk = _project(xb, w_dqk, seq, "rope_partial", DIFF_QK_DIM, diff_tab,
                   (DIFF_QK_DIM ** -0.5 * LOG2_E, 1.0), DQ_COLS)
    dvt = _project(xb, w_dv, seq, "plain_t", DIFF_V_DIM)

    ro = _retention(rqk, rvg, dec_f, dec_b, gn_w, batch, seq)
    do = _diff_attention(dqk, dvt, lq1, lk1, lq2, lk2, subln_w, batch, seq, lambda_init)

    x1, x1b = _out_projection(ro, do, w_out, x2, ln1_g, ln1_b)
    h = _gate_up(x1b, wg, wu)
    y = _down_projection(h, wd, x1, ln2_g, ln2_b)
    return y.reshape(batch, seq, d)


def kernel(x_prompt, x_sample, w_in, ret_decay_f, ret_decay_b, ret_gn_w, diff_lambda_q1,
           diff_lambda_k1, diff_lambda_q2, diff_lambda_k2, diff_subln_w, w_out, ln1_g, ln1_b,
           w_gate, w_up, w_down, ln2_g, ln2_b):
    y_prompt, y_sample = x_prompt, x_sample
    for l in range(DEPTH):
        lambda_init = 0.8 - 0.6 * math.exp(-0.3 * l)
        weights = _prepare_weights(w_in[l], w_out[l], w_gate[l], w_up[l], w_down[l])
        tables = _rope_tables(max(y_prompt.shape[1], y_sample.shape[1]))
        rest = (tables, ret_decay_f[l], ret_decay_b[l], ret_gn_w[l], diff_lambda_q1[l], diff_lambda_k1[l],
                diff_lambda_q2[l], diff_lambda_k2[l], diff_subln_w[l], ln1_g[l], ln1_b[l],
                ln2_g[l], ln2_b[l], lambda_init)
        y_prompt = _encoder_layer(y_prompt, weights, *rest)
        y_sample = _encoder_layer(y_sample, weights, *rest)
    return (y_prompt, y_sample)
```

```python
import functools
import math

import jax
import jax.numpy as jnp
from jax import lax
from jax.experimental import pallas as pl
from jax.experimental.pallas import tpu as pltpu

D_MODEL = 4096
DEPTH = 1
RET_HEADS = 8
RET_QK_DIM = 128
RET_V_DIM = 256
RET_ROT_BASE = 10000.0
DIFF_HEADS = 8
DIFF_QK_DIM = 128
DIFF_V_DIM = 256
DIFF_ROT_DIM = DIFF_QK_DIM // 4
ROPE_THETA = 500000.0
D_FF = 11008
DEEPNORM_ALPHA = (2.0 * DEPTH) ** 0.25
LN_EPS = 1e-5
GN_EPS = 1e-6
RMS_EPS = 1e-5

RQ_COLS = RET_HEADS * RET_QK_DIM
RV_COLS = RET_HEADS * RET_V_DIM
DQ_COLS = DIFF_HEADS * 2 * DIFF_QK_DIM
DV_COLS = DIFF_HEADS * DIFF_V_DIM

LANES = 128
VMEM_LIMIT_BYTES = 56 * 1024 * 1024
D_FF_PAD = 11264
PROJ_ROWS = 1024
PROJ_COLS = 512
OUT_ROWS = 256
OUT_COL_GROUPS = 4
GATE_UP_ROWS = 1024
GATE_UP_COLS = 512
DOWN_ROWS = 1024
DOWN_K = 512
RET_CHUNK = 256
RET_BLOCK = 2048
ATTN_Q_ROWS = 512
ATTN_Q_BLOCKS = 2
ATTN_KV_UNROLL = 4
DOWN_COL_CHUNK = 1024
LN_ROW_CHUNK = 64
LOG2_E = math.log2(math.e)

BF16 = jnp.bfloat16
F32 = jnp.float32

NT_DIMS = (((1,), (1,)), ((), ()))
TN_DIMS = (((0,), (0,)), ((), ()))


def _tile(n, pref):
    t = pref
    while t > 8 and n % t:
        t //= 2
    assert n % t == 0, (n, pref)
    return t


def _params(sem):
    return pltpu.CompilerParams(dimension_semantics=sem, vmem_limit_bytes=VMEM_LIMIT_BYTES)


def _swish(g):
    return g * (1.0 / (1.0 + jnp.exp(-g)))


def _layer_norm(pre, g, b):
    mu = jnp.mean(pre, axis=-1, keepdims=True)
    var = jnp.mean(jnp.square(pre - mu), axis=-1, keepdims=True)
    return (pre - mu) * lax.rsqrt(var + LN_EPS) * g + b


def _three_phase(i, n, first, middle, last):
    pl.when(i == 0)(first)
    pl.when(jnp.logical_and(i > 0, i < n))(middle)
    pl.when(i == n)(last)


def _proj_kernel(x_ref, w_ref, *rest, mode, hw, scales, split):
    if mode in ("rope_full", "rope_partial"):
        tab_ref, o_ref = rest
        scale = jnp.where(pl.program_id(1) < split, scales[0], scales[1]).astype(F32)
    else:
        (o_ref,) = rest
    tm = x_ref.shape[0]
    n_split = 1 if mode == "plain" else 2
    rows = tm // n_split
    for r in range(n_split):
        rs = slice(r * rows, (r + 1) * rows)
        acc = jnp.dot(x_ref[rs, :], w_ref[...], preferred_element_type=F32)
        for c in range(o_ref.shape[0]):
            blk = acc[:, c * hw:(c + 1) * hw]
            if mode == "rope_full":
                cos = tab_ref[rs, 0:LANES]
                sin = tab_ref[rs, LANES:2 * LANES]
                blk = (blk * cos + pltpu.roll(blk, LANES // 2, 1) * sin) * scale
            elif mode == "rope_partial":
                half = DIFF_ROT_DIM // 2
                a = tab_ref[rs, 0:LANES]
                b = tab_ref[rs, LANES:2 * LANES]
                c2 = tab_ref[rs, 2 * LANES:3 * LANES]
                blk = (blk * a + pltpu.roll(blk, half, 1) * b
                       + pltpu.roll(blk, LANES - half, 1) * c2) * scale
            if mode == "plain_t":
                o_ref[c, :, rs] = blk.T.astype(o_ref.dtype)
            else:
                o_ref[c, rs, :] = blk.astype(o_ref.dtype)


def _project(x, w, seq, mode, hw, tab=None, scales=None, split_cols=None):
    t, d = x.shape
    n = w.shape[1]
    tm = _tile(seq, PROJ_ROWS)
    tn = _tile(n, PROJ_COLS)
    grid = (t // tm, n // tn)
    in_specs = [pl.BlockSpec((tm, d), lambda i, j: (i, 0)),
                pl.BlockSpec((d, tn), lambda i, j: (0, j))]
    args = [x, w]
    if tab is not None:
        per_seq = seq // tm
        in_specs.append(pl.BlockSpec((tm, tab.shape[1]), lambda i, j: (i % per_seq, 0)))
        args.append(tab)
    if mode == "plain_t":
        out_spec = pl.BlockSpec((tn // hw, None, hw, tm), lambda i, j: (j, i, 0, 0))
        out_shape = jax.ShapeDtypeStruct((n // hw, t // tm, hw, tm), BF16)
    else:
        out_spec = pl.BlockSpec((tn // hw, tm, hw), lambda i, j: (j, i, 0))
        out_shape = jax.ShapeDtypeStruct((n // hw, t, hw), BF16)
    return pl.pallas_call(
        functools.partial(_proj_kernel, mode=mode, hw=hw, scales=scales,
                          split=None if split_cols is None else split_cols // tn),
        grid=grid,
        in_specs=in_specs,
        out_specs=out_spec,
        out_shape=out_shape,
        compiler_params=_params(("parallel", "arbitrary")),
        name="proj_" + mode,
    )(*args)


def _ret_kernel(decf_ref, decb_ref, q_ref, k_ref, v_ref, g_ref, gnw_ref, o_ref,
                sf_scr, sb_scr, sball_scr, *, nblk, nb, chunk):
    s = pl.program_id(2)
    lg_f = -jnp.exp(decf_ref[...])
    lg_b = -jnp.exp(decb_ref[...])
    row = lax.broadcasted_iota(jnp.int32, (chunk, 1), 0).astype(F32)

    @pl.when(s == 0)
    def _():
        sb_scr[...] = jnp.zeros_like(sb_scr)

    @pl.when(s < nblk)
    def _():
        blk = nblk - 1 - s
        zeta_b = jnp.exp(row * lg_b)
        decay_b = jnp.exp(chunk * lg_b)
        state = sb_scr[...]
        for ci in reversed(range(nb)):
            rs = slice(ci * chunk, (ci + 1) * chunk)
            sball_scr[blk * nb + ci] = state.astype(BF16)
            kz = (k_ref[rs, :].astype(F32) * zeta_b).astype(BF16)
            state = decay_b * state + lax.dot_general(kz, v_ref[rs, :], TN_DIMS,
                                                      preferred_element_type=F32)
        sb_scr[...] = state

    @pl.when(s == nblk)
    def _():
        sf_scr[...] = jnp.zeros_like(sf_scr)

    @pl.when(s >= nblk)
    def _():
        blk = s - nblk
        rel = (lax.broadcasted_iota(jnp.int32, (chunk, chunk), 0)
               - lax.broadcasted_iota(jnp.int32, (chunk, chunk), 1)).astype(F32)
        dmat = (jnp.where(rel >= 0, jnp.exp(jnp.maximum(rel, 0.0) * lg_f), 0.0)
                + jnp.where(rel <= 0, jnp.exp(jnp.maximum(-rel, 0.0) * lg_b), 0.0))
        xi_f = jnp.exp((row + 1.0) * lg_f)
        xi_b = jnp.exp((chunk - row) * lg_b)
        zeta_f = jnp.exp((chunk - 1.0 - row) * lg_f)
        decay_f = jnp.exp(chunk * lg_f)
        gnw = gnw_ref[...]
        state = sf_scr[...]
        for ci in range(nb):
            rs = slice(ci * chunk, (ci + 1) * chunk)
            q = q_ref[rs, :]
            k = k_ref[rs, :]
            v = v_ref[rs, :]
            sc = lax.dot_general(q, k, NT_DIMS, preferred_element_type=F32)
            qf = q.astype(F32)
            o = jnp.dot((sc * dmat).astype(BF16), v, preferred_element_type=F32)
            o += jnp.dot((qf * xi_f).astype(BF16), state.astype(BF16), preferred_element_type=F32)
            o += jnp.dot((qf * xi_b).astype(BF16), sball_scr[blk * nb + ci],
                         preferred_element_type=F32)
            kz = (k.astype(F32) * zeta_f).astype(BF16)
            state = decay_f * state + lax.dot_general(kz, v, TN_DIMS, preferred_element_type=F32)
            mu = jnp.mean(o, axis=-1, keepdims=True)
            var = jnp.mean(jnp.square(o - mu), axis=-1, keepdims=True)
            on = (o - mu) * lax.rsqrt(var + GN_EPS) * gnw
            o_ref[rs, :] = (_swish(g_ref[rs, :].astype(F32)) * on).astype(o_ref.dtype)
        sf_scr[...] = state


def _retention(rqk, rvg, dec_f, dec_b, gn_w, batch, seq):
    t = rqk.shape[1]
    rows = _tile(seq, RET_BLOCK)
    chunk = _tile(rows, RET_CHUNK)
    nb = rows // chunk
    nblk = seq // rows

    def bidx(b, s):
        return b * nblk + jnp.where(s < nblk, nblk - 1 - s, s - nblk)

    def oidx(b, s):
        return b * nblk + jnp.maximum(s - nblk, 0)

    return pl.pallas_call(
        functools.partial(_ret_kernel, nblk=nblk, nb=nb, chunk=chunk),
        grid=(batch, RET_HEADS, 2 * nblk),
        in_specs=[
            pl.BlockSpec((None, 1, 1), lambda b, h, s: (h, 0, 0)),
            pl.BlockSpec((None, 1, 1), lambda b, h, s: (h, 0, 0)),
            pl.BlockSpec((None, rows, RET_QK_DIM), lambda b, h, s: (h, oidx(b, s), 0)),
            pl.BlockSpec((None, rows, RET_QK_DIM), lambda b, h, s: (RET_HEADS + h, bidx(b, s), 0)),
            pl.BlockSpec((None, rows, RET_V_DIM), lambda b, h, s: (h, bidx(b, s), 0)),
            pl.BlockSpec((None, rows, RET_V_DIM), lambda b, h, s: (RET_HEADS + h, oidx(b, s), 0)),
            pl.BlockSpec((None, 1, RET_V_DIM), lambda b, h, s: (h, 0, 0)),
        ],
        out_specs=pl.BlockSpec((rows, RET_V_DIM), lambda b, h, s: (oidx(b, s), h)),
        out_shape=jax.ShapeDtypeStruct((t, RV_COLS), BF16),
        scratch_shapes=[
            pltpu.VMEM((RET_QK_DIM, RET_V_DIM), F32),
            pltpu.VMEM((RET_QK_DIM, RET_V_DIM), F32),
            pltpu.VMEM((nblk * nb, RET_QK_DIM, RET_V_DIM), BF16),
        ],
        compiler_params=_params(("parallel", "parallel", "arbitrary")),
        name="retention",
    )(dec_f.reshape(RET_HEADS, 1, 1), dec_b.reshape(RET_HEADS, 1, 1),
      rqk, rqk, rvg, rvg, gn_w.reshape(RET_HEADS, 1, RET_V_DIM))


def _diff_kernel(lq1_ref, lk1_ref, lq2_ref, lk2_ref, sub_ref, q_ref, k_ref, vt_ref, o_ref,
                 acc_scr, s0_scr, s1_scr, m0_scr, m1_scr, *, n_kv, tk, tq, lambda_init):
    n_blocks = q_ref.shape[1] // tq
    s0 = (s0_scr, m0_scr)
    s1 = (s1_scr, m1_scr)
    bufs = (s0, s1)
    lam = (jnp.exp(jnp.sum(lq1_ref[...] * lk1_ref[...], axis=1, keepdims=True))
           - jnp.exp(jnp.sum(lq2_ref[...] * lk2_ref[...], axis=1, keepdims=True)) + lambda_init)

    def scores(r, j, buf):
        s_scr, m_scr = buf
        start = pl.multiple_of(j * tk, tk)
        for a in range(2):
            st = lax.dot_general(k_ref[a, pl.ds(start, tk), :],
                                 q_ref[a, r * tq:(r + 1) * tq, :], NT_DIMS,
                                 preferred_element_type=F32)
            s_scr[a] = st
            m_scr[a] = jnp.max(st, axis=0, keepdims=True)

    def absorb(j, buf, carry):
        s_scr, m_scr = buf
        vt = vt_ref[j]
        out = []
        for a in range(2):
            m_prev, l_prev = carry[a]
            st = s_scr[a]
            m_new = jnp.maximum(m_prev, m_scr[a])
            alpha = jnp.exp2(m_prev - m_new)
            p = jnp.exp2(st - m_new)
            l_new = alpha * l_prev + jnp.sum(p, axis=0, keepdims=True)
            acc_scr[a] = alpha * acc_scr[a] + jnp.dot(vt, p.astype(BF16),
                                                      preferred_element_type=F32)
            out.append((m_new, l_new))
        return tuple(out)

    def finalize(r, carry):
        (_, l0), (_, l1) = carry
        ot = acc_scr[0] * (1.0 / l0) - lam * (acc_scr[1] * (1.0 / l1))
        o = ot.T
        o = o * lax.rsqrt(jnp.mean(jnp.square(o), axis=-1, keepdims=True) + RMS_EPS)
        o_ref[r * tq:(r + 1) * tq, :] = (o * sub_ref[...] * (1.0 - lambda_init)).astype(o_ref.dtype)

    init = (jnp.full((1, tq), -jnp.inf, F32), jnp.zeros((1, tq), F32))
    unroll = next(u for u in (ATTN_KV_UNROLL, 2, 1) if n_kv % u == 0)

    def run(r, base, first, carry, last):
        for u in range(unroll):
            nxt = bufs[(first + u + 1) % 2]
            if not last or u + 1 < unroll:
                scores(r, base + u + 1, nxt)
            elif r + 1 < n_blocks:
                scores(r + 1, 0, nxt)
            carry = absorb(base + u, bufs[(first + u) % 2], carry)
        return carry

    scores(0, 0, s0)
    for r in range(n_blocks):
        first = (r * n_kv) % 2
        acc_scr[...] = jnp.zeros_like(acc_scr)
        carry = lax.fori_loop(
            0, n_kv // unroll - 1,
            lambda i, c, r=r, first=first: run(r, i * unroll, first, c, False), (init, init))
        carry = run(r, n_kv - unroll, first, carry, True)
        finalize(r, carry)


def _diff_attention(dqk, dvt, lq1, lk1, lq2, lk2, subln_w, batch, seq, lambda_init):
    t = dqk.shape[1]
    tk = dvt.shape[3]
    n_kv = seq // tk
    assert n_kv == 1 or n_kv % 2 == 0, n_kv
    tq = _tile(seq, ATTN_Q_ROWS)
    rows = _tile(seq, ATTN_Q_ROWS * ATTN_Q_BLOCKS)
    nq = seq // rows
    vec = lambda a: a.reshape(1, -1).astype(F32)
    vspec = lambda n: pl.BlockSpec((1, n), lambda b, h, i: (0, 0))
    return pl.pallas_call(
        functools.partial(_diff_kernel, n_kv=n_kv, tk=tk, tq=tq, lambda_init=lambda_init),
        grid=(batch, DIFF_HEADS, nq),
        in_specs=[
            vspec(DIFF_QK_DIM), vspec(DIFF_QK_DIM), vspec(DIFF_QK_DIM), vspec(DIFF_QK_DIM),
            vspec(DIFF_V_DIM),
            pl.BlockSpec((2, rows, DIFF_QK_DIM), lambda b, h, i: (h, b * nq + i, 0)),
            pl.BlockSpec((2, seq, DIFF_QK_DIM), lambda b, h, i: (DIFF_HEADS + h, b, 0)),
            pl.BlockSpec((None, n_kv, DIFF_V_DIM, tk), lambda b, h, i: (h, b, 0, 0)),
        ],
        out_specs=pl.BlockSpec((rows, DIFF_V_DIM), lambda b, h, i: (b * nq + i, h)),
        out_shape=jax.ShapeDtypeStruct((t, DV_COLS), BF16),
        scratch_shapes=[pltpu.VMEM((2, DIFF_V_DIM, tq), F32),
                        pltpu.VMEM((2, tk, tq), F32),
                        pltpu.VMEM((2, tk, tq), F32),
                        pltpu.VMEM((2, 1, tq), F32),
                        pltpu.VMEM((2, 1, tq), F32)],
        compiler_params=_params(("parallel", "parallel", "arbitrary")),
        name="diff_attention",
    )(vec(lq1), vec(lk1), vec(lq2), vec(lk2), vec(subln_w), dqk, dqk, dvt)


def _outproj_kernel(ro_ref, do_ref, w_ref, x_ref, g_ref, b_ref, o32_ref, o16_ref,
                    mix_scr, pre_scr, *, ni, nj):
    i = pl.program_id(0)
    j = pl.program_id(1)
    half = ro_ref.shape[1]
    rows = o32_ref.shape[0]

    @pl.when(jnp.logical_and(j == 0, i < ni))
    def _():
        mix_scr[:, :half] = ro_ref[...]
        mix_scr[:, half:] = do_ref[...]

    def produce():
        pre_scr[i % 2, j] = jnp.dot(mix_scr[...], w_ref[j], preferred_element_type=F32)

    def normalise():
        slot = (i + 1) % 2
        rs = pl.ds(pl.multiple_of(j * rows, rows), rows)
        pre = jnp.concatenate([pre_scr[slot, c, rs, :] for c in range(nj)], axis=1)
        pre = pre + DEEPNORM_ALPHA * x_ref[...]
        y = _layer_norm(pre, g_ref[...], b_ref[...])
        o32_ref[...] = y
        o16_ref[...] = y.astype(o16_ref.dtype)

    def both():
        normalise()
        produce()

    _three_phase(i, ni, produce, both, normalise)


def _out_projection(ro, do, w_out, x, ln_g, ln_b):
    t = x.shape[0]
    tm = _tile(t, OUT_ROWS)
    nj, _, tn = w_out.shape
    ni = t // tm
    rows = tm // nj
    half = RV_COLS
    tile = lambda i: jnp.minimum(i, ni - 1)
    prev = lambda i, j: (jnp.where(i == 0, 0, (i - 1) * nj + j), 0)
    return pl.pallas_call(
        functools.partial(_outproj_kernel, ni=ni, nj=nj),
        grid=(ni + 1, nj),
        in_specs=[
            pl.BlockSpec((tm, half), lambda i, j: (tile(i), 0)),
            pl.BlockSpec((tm, half), lambda i, j: (tile(i), 0)),
            pl.BlockSpec((nj, 2 * half, tn), lambda i, j: (0, 0, 0)),
            pl.BlockSpec((rows, D_MODEL), prev),
            pl.BlockSpec((1, D_MODEL), lambda i, j: (0, 0)),
            pl.BlockSpec((1, D_MODEL), lambda i, j: (0, 0)),
        ],
        out_specs=[pl.BlockSpec((rows, D_MODEL), prev), pl.BlockSpec((rows, D_MODEL), prev)],
        out_shape=[jax.ShapeDtypeStruct((t, D_MODEL), F32),
                   jax.ShapeDtypeStruct((t, D_MODEL), BF16)],
        scratch_shapes=[pltpu.VMEM((tm, 2 * half), BF16),
                        pltpu.VMEM((2, nj, tm, tn), F32)],
        compiler_params=_params(("arbitrary", "arbitrary")),
        name="out_projection",
    )(ro, do, w_out, x, ln_g.reshape(1, -1), ln_b.reshape(1, -1))


def _gate_up_kernel(x_ref, wg_ref, wu_ref, o_ref):
    x = x_ref[...]
    g = jnp.dot(x, wg_ref[...], preferred_element_type=F32)
    u = jnp.dot(x, wu_ref[...], preferred_element_type=F32)
    o_ref[...] = (_swish(g) * u).astype(o_ref.dtype)


def _gate_up(xb, wg, wu):
    t, d = xb.shape
    n = wg.shape[1]
    tm = _tile(t, GATE_UP_ROWS)
    tn = GATE_UP_COLS
    return pl.pallas_call(
        _gate_up_kernel,
        grid=(t // tm, n // tn),
        in_specs=[pl.BlockSpec((tm, d), lambda i, j: (i, 0)),
                  pl.BlockSpec((d, tn), lambda i, j: (0, j)),
                  pl.BlockSpec((d, tn), lambda i, j: (0, j))],
        out_specs=pl.BlockSpec((tm, tn), lambda i, j: (i, j)),
        out_shape=jax.ShapeDtypeStruct((t, n), BF16),
        compiler_params=_params(("parallel", "arbitrary")),
        name="ffn_gate_up",
    )(xb, wg, wu)


def _down_kernel(h_ref, w_ref, x_ref, g_ref, b_ref, o_ref, acc_scr, *, ni, n_ln):
    i = pl.program_id(0)
    k = pl.program_id(1)
    rows = o_ref.shape[0]
    width = o_ref.shape[1]

    def produce():
        h = h_ref[...]
        slot = i % 2
        for c in range(width // DOWN_COL_CHUNK):
            sl = slice(c * DOWN_COL_CHUNK, (c + 1) * DOWN_COL_CHUNK)
            part = jnp.dot(h, w_ref[:, sl], preferred_element_type=F32)
            acc_scr[slot, :, sl] = jnp.where(k > 0, acc_scr[slot, :, sl], 0.0) + part

    def normalise():
        slot = (i + 1) % 2
        chunk = jnp.minimum(k, n_ln - 1)
        rs = pl.ds(pl.multiple_of(chunk * rows, rows), rows)
        pre = DEEPNORM_ALPHA * x_ref[...] + acc_scr[slot, rs, :]
        o_ref[...] = _layer_norm(pre, g_ref[...], b_ref[...])

    def first():
        @pl.when(k == 0)
        def _():
            acc_scr[...] = jnp.zeros_like(acc_scr)

        produce()

    def both():
        normalise()
        produce()

    _three_phase(i, ni, first, both, normalise)


def _down_projection(h, wd, x1, ln_g, ln_b):
    t, kdim = h.shape
    tm = _tile(t, DOWN_ROWS)
    tk = DOWN_K
    ni = t // tm
    nk = kdim // tk
    rows = LN_ROW_CHUNK
    n_ln = tm // rows
    assert n_ln <= nk, (n_ln, nk)
    tile = lambda i: jnp.minimum(i, ni - 1)
    prev = lambda i, k: (jnp.where(i == 0, 0, (i - 1) * n_ln + jnp.minimum(k, n_ln - 1)), 0)
    return pl.pallas_call(
        functools.partial(_down_kernel, ni=ni, n_ln=n_ln),
        grid=(ni + 1, nk),
        in_specs=[pl.BlockSpec((tm, tk), lambda i, k: (tile(i), k)),
                  pl.BlockSpec((tk, D_MODEL), lambda i, k: (k, 0)),
                  pl.BlockSpec((rows, D_MODEL), prev),
                  pl.BlockSpec((1, D_MODEL), lambda i, k: (0, 0)),
                  pl.BlockSpec((1, D_MODEL), lambda i, k: (0, 0))],
        out_specs=pl.BlockSpec((rows, D_MODEL), prev),
        out_shape=jax.ShapeDtypeStruct((t, D_MODEL), F32),
        scratch_shapes=[pltpu.VMEM((2, tm, D_MODEL), F32)],
        compiler_params=_params(("arbitrary", "arbitrary")),
        name="ffn_down",
    )(h, wd, x1, ln_g.reshape(1, -1), ln_b.reshape(1, -1))


def _rope_tables(seq):
    pos = jnp.arange(seq, dtype=F32)

    def cos_sin(rot_dim, base):
        inv_freq = base ** (-jnp.arange(0, rot_dim, 2, dtype=F32) / rot_dim)
        ang = pos[:, None] * inv_freq[None, :]
        return jnp.cos(ang), jnp.sin(ang)

    c, s = cos_sin(RET_QK_DIM, RET_ROT_BASE)
    ret_tab = jnp.concatenate([c, c, -s, s], axis=-1)

    c, s = cos_sin(DIFF_ROT_DIM, ROPE_THETA)
    half = DIFF_ROT_DIM // 2
    rest = DIFF_QK_DIM - DIFF_ROT_DIM
    ones = jnp.ones((seq, rest), F32)
    zeros = jnp.zeros((seq, rest), F32)
    zh = jnp.zeros((seq, half), F32)
    diff_tab = jnp.concatenate([c, c, ones, zh, s, zeros, -s, zh, zeros], axis=-1)
    return ret_tab, diff_tab


def _prepare_weights(w_in, w_out, w_gate, w_up, w_down):
    wb = w_in.astype(BF16)
    c0 = 2 * RQ_COLS
    c1 = c0 + 2 * RV_COLS
    c2 = c1 + 2 * DQ_COLS
    pad = D_FF_PAD - D_FF
    zc = jnp.zeros((w_gate.shape[0], pad), BF16)
    wg = jnp.concatenate([w_gate.astype(BF16), zc], axis=1)
    wu = jnp.concatenate([w_up.astype(BF16), zc], axis=1)
    wd = jnp.concatenate([w_down.astype(BF16), jnp.zeros((pad, w_down.shape[1]), BF16)], axis=0)
    k, n = w_out.shape
    wo = w_out.astype(BF16).reshape(k, OUT_COL_GROUPS, n // OUT_COL_GROUPS).transpose(1, 0, 2)
    return wb[:, :c0], wb[:, c0:c1], wb[:, c1:c2], wb[:, c2:], wo, wg, wu, wd


def _encoder_layer(x, weights, tables, dec_f, dec_b, gn_w, lq1, lk1, lq2, lk2, subln_w,
                   ln1_g, ln1_b, ln2_g, ln2_b, lambda_init):
    batch, seq, d = x.shape
    w_rqk, w_rvg, w_dqk, w_dv, w_out, wg, wu, wd = weights
    x2 = x.reshape(batch * seq, d)
    ret_tab, diff_tab = tables

    xb = x2.astype(BF16)
    rvg = _project(xb, w_rvg, seq, "plain", RET_V_DIM)
    rqk = _project(xb, w_rqk, seq, "rope_full", RET_QK_DIM, ret_tab,
                   (1.0, RET_QK_DIM ** -0.5), RQ_COLS)
    dqk = _project(xb, w_dqk, seq, "rope_partial", DIFF_QK_DIM, diff_tab,
                   (DIFF_QK_DIM ** -0.5 * LOG2_E, 1.0), DQ_COLS)
    dvt = _project(xb, w_dv, seq, "plain_t", DIFF_V_DIM)

    ro = _retention(rqk, rvg, dec_f, dec_b, gn_w, batch, seq)
    do = _diff_attention(dqk, dvt, lq1, lk1, lq2, lk2, subln_w, batch, seq, lambda_init)

    x1, x1b = _out_projection(ro, do, w_out, x2, ln1_g, ln1_b)
    h = _gate_up(x1b, wg, wu)
    y = _down_projection(h, wd, x1, ln2_g, ln2_b)
    return y.reshape(batch, seq, d)


def kernel(x_prompt, x_sample, w_in, ret_decay_f, ret_decay_b, ret_gn_w, diff_lambda_q1,
           diff_lambda_k1, diff_lambda_q2, diff_lambda_k2, diff_subln_w, w_out, ln1_g, ln1_b,
           w_gate, w_up, w_down, ln2_g, ln2_b):
    y_prompt, y_sample = x_prompt, x_sample
    for l in range(DEPTH):
        lambda_init = 0.8 - 0.6 * math.exp(-0.3 * l)
        weights = _prepare_weights(w_in[l], w_out[l], w_gate[l], w_up[l], w_down[l])
        tables = _rope_tables(max(y_prompt.shape[1], y_sample.shape[1]))
        rest = (tables, ret_decay_f[l], ret_decay_b[l], ret_gn_w[l], diff_lambda_q1[l], diff_lambda_k1[l],
                diff_lambda_q2[l], diff_lambda_k2[l], diff_subln_w[l], ln1_g[l], ln1_b[l],
                ln2_g[l], ln2_b[l], lambda_init)
        y_prompt = _encoder_layer(y_prompt, weights, *rest)
        y_sample = _encoder_layer(y_sample, weights, *rest)
    return (y_prompt, y_sample)
```

```python
import functools
import math

import jax
import jax.numpy as jnp
from jax import lax
from jax.experimental import pallas as pl
from jax.experimental.pallas import tpu as pltpu

D_MODEL = 4096
DEPTH = 1
RET_HEADS = 8
RET_QK_DIM = 128
RET_V_DIM = 256
RET_ROT_BASE = 10000.0
DIFF_HEADS = 8
DIFF_QK_DIM = 128
DIFF_V_DIM = 256
DIFF_ROT_DIM = DIFF_QK_DIM // 4
ROPE_THETA = 500000.0
D_FF = 11008
DEEPNORM_ALPHA = (2.0 * DEPTH) ** 0.25
LN_EPS = 1e-5
GN_EPS = 1e-6
RMS_EPS = 1e-5

RQ_COLS = RET_HEADS * RET_QK_DIM
RV_COLS = RET_HEADS * RET_V_DIM
DQ_COLS = DIFF_HEADS * 2 * DIFF_QK_DIM
DV_COLS = DIFF_HEADS * DIFF_V_DIM

LANES = 128
VMEM_LIMIT_BYTES = 56 * 1024 * 1024
D_FF_PAD = 11264
PROJ_ROWS = 1024
PROJ_COLS = 512
PROJ_ROW_SPLITS = {"plain": 1, "plain_t": 2, "rope_full": 2, "rope_partial": 4}
OUT_ROWS = 256
OUT_COL_GROUPS = 4
GATE_UP_ROWS = 1024
GATE_UP_COLS = 512
DOWN_ROWS = 1024
DOWN_K = 512
RET_CHUNK = 256
RET_BLOCK = 4096
ATTN_Q_ROWS = 512
ATTN_Q_BLOCKS = 2
ATTN_KV_UNROLL = 4
DOWN_COL_CHUNK = 1024
LN_ROW_CHUNK = 64
LOG2_E = math.log2(math.e)

BF16 = jnp.bfloat16
F32 = jnp.float32

NT_DIMS = (((1,), (1,)), ((), ()))
TN_DIMS = (((0,), (0,)), ((), ()))


def _tile(n, pref):
    t = pref
    while t > 8 and n % t:
        t //= 2
    assert n % t == 0, (n, pref)
    return t


def _params(sem):
    return pltpu.CompilerParams(dimension_semantics=sem, vmem_limit_bytes=VMEM_LIMIT_BYTES)


def _swish(g):
    return g * (1.0 / (1.0 + jnp.exp(-g)))


def _layer_norm(pre, g, b):
    mu = jnp.mean(pre, axis=-1, keepdims=True)
    var = jnp.mean(jnp.square(pre - mu), axis=-1, keepdims=True)
    return (pre - mu) * lax.rsqrt(var + LN_EPS) * g + b


def _three_phase(i, n, first, middle, last):
    pl.when(i == 0)(first)
    pl.when(jnp.logical_and(i > 0, i < n))(middle)
    pl.when(i == n)(last)


def _proj_kernel(x_ref, w_ref, *rest, mode, hw, scales, split):
    if mode in ("rope_full", "rope_partial"):
        tab_ref, o_ref = rest
        scale = jnp.where(pl.program_id(1) < split, scales[0], scales[1]).astype(F32)
    else:
        (o_ref,) = rest
    tm = x_ref.shape[0]
    n_split = PROJ_ROW_SPLITS[mode]
    rows = tm // n_split
    for r in range(n_split):
        rs = slice(r * rows, (r + 1) * rows)
        acc = jnp.dot(x_ref[rs, :], w_ref[...], preferred_element_type=F32)
        for c in range(o_ref.shape[0]):
            blk = acc[:, c * hw:(c + 1) * hw]
            if mode == "rope_full":
                cos = tab_ref[rs, 0:LANES]
                sin = tab_ref[rs, LANES:2 * LANES]
                blk = (blk * cos + pltpu.roll(blk, LANES // 2, 1) * sin) * scale
            elif mode == "rope_partial":
                half = DIFF_ROT_DIM // 2
                a = tab_ref[rs, 0:LANES]
                b = tab_ref[rs, LANES:2 * LANES]
                c2 = tab_ref[rs, 2 * LANES:3 * LANES]
                blk = (blk * a + pltpu.roll(blk, half, 1) * b
                       + pltpu.roll(blk, LANES - half, 1) * c2) * scale
            if mode == "plain_t":
                o_ref[c, :, rs] = blk.T.astype(o_ref.dtype)
            else:
                o_ref[c, rs, :] = blk.astype(o_ref.dtype)


def _project(x, w, seq, mode, hw, tab=None, scales=None, split_cols=None):
    t, d = x.shape
    n = w.shape[1]
    tm = _tile(seq, PROJ_ROWS)
    tn = _tile(n, PROJ_COLS)
    grid = (t // tm, n // tn)
    in_specs = [pl.BlockSpec((tm, d), lambda i, j: (i, 0)),
                pl.BlockSpec((d, tn), lambda i, j: (0, j))]
    args = [x, w]
    if tab is not None:
        per_seq = seq // tm
        in_specs.append(pl.BlockSpec((tm, tab.shape[1]), lambda i, j: (i % per_seq, 0)))
        args.append(tab)
    if mode == "plain_t":
        out_spec = pl.BlockSpec((tn // hw, None, hw, tm), lambda i, j: (j, i, 0, 0))
        out_shape = jax.ShapeDtypeStruct((n // hw, t // tm, hw, tm), BF16)
    else:
        out_spec = pl.BlockSpec((tn // hw, tm, hw), lambda i, j: (j, i, 0))
        out_shape = jax.ShapeDtypeStruct((n // hw, t, hw), BF16)
    return pl.pallas_call(
        functools.partial(_proj_kernel, mode=mode, hw=hw, scales=scales,
                          split=None if split_cols is None else split_cols // tn),
        grid=grid,
        in_specs=in_specs,
        out_specs=out_spec,
        out_shape=out_shape,
        compiler_params=_params(("parallel", "arbitrary")),
        name="proj_" + mode,
    )(*args)


def _ret_kernel(decf_ref, decb_ref, q_ref, k_ref, v_ref, g_ref, gnw_ref, o_ref,
                sf_scr, sb_scr, sball_scr, *, nblk, nb, chunk):
    s = pl.program_id(2)
    lg_f = -jnp.exp(decf_ref[...])
    lg_b = -jnp.exp(decb_ref[...])
    row = lax.broadcasted_iota(jnp.int32, (chunk, 1), 0).astype(F32)

    @pl.when(s == 0)
    def _():
        sb_scr[...] = jnp.zeros_like(sb_scr)

    @pl.when(s < nblk)
    def _():
        blk = nblk - 1 - s
        zeta_b = jnp.exp(row * lg_b)
        decay_b = jnp.exp(chunk * lg_b)
        state = sb_scr[...]
        for ci in reversed(range(nb)):
            rs = slice(ci * chunk, (ci + 1) * chunk)
            sball_scr[blk * nb + ci] = state.astype(BF16)
            kz = (k_ref[rs, :].astype(F32) * zeta_b).astype(BF16)
            state = decay_b * state + lax.dot_general(kz, v_ref[rs, :], TN_DIMS,
                                                      preferred_element_type=F32)
        sb_scr[...] = state

    @pl.when(s == nblk)
    def _():
        sf_scr[...] = jnp.zeros_like(sf_scr)

    @pl.when(s >= nblk)
    def _():
        blk = s - nblk
        rel = (lax.broadcasted_iota(jnp.int32, (chunk, chunk), 0)
               - lax.broadcasted_iota(jnp.int32, (chunk, chunk), 1)).astype(F32)
        dmat = (jnp.where(rel >= 0, jnp.exp(jnp.maximum(rel, 0.0) * lg_f), 0.0)
                + jnp.where(rel <= 0, jnp.exp(jnp.maximum(-rel, 0.0) * lg_b), 0.0))
        xi_f = jnp.exp((row + 1.0) * lg_f)
        xi_b = jnp.exp((chunk - row) * lg_b)
        zeta_f = jnp.exp((chunk - 1.0 - row) * lg_f)
        decay_f = jnp.exp(chunk * lg_f)
        gnw = gnw_ref[...]
        state = sf_scr[...]
        for ci in range(nb):
            rs = slice(ci * chunk, (ci + 1) * chunk)
            q = q_ref[rs, :]
            k = k_ref[rs, :]
            v = v_ref[rs, :]
            sc = lax.dot_general(q, k, NT_DIMS, preferred_element_type=F32)
            qf = q.astype(F32)
            o = jnp.dot((sc * dmat).astype(BF16), v, preferred_element_type=F32)
            o += jnp.dot((qf * xi_f).astype(BF16), state.astype(BF16), preferred_element_type=F32)
            o += jnp.dot((qf * xi_b).astype(BF16), sball_scr[blk * nb + ci],
                         preferred_element_type=F32)
            kz = (k.astype(F32) * zeta_f).astype(BF16)
            state = decay_f * state + lax.dot_general(kz, v, TN_DIMS, preferred_element_type=F32)
            mu = jnp.mean(o, axis=-1, keepdims=True)
            var = jnp.mean(jnp.square(o - mu), axis=-1, keepdims=True)
            on = (o - mu) * lax.rsqrt(var + GN_EPS) * gnw
            o_ref[rs, :] = (_swish(g_ref[rs, :].astype(F32)) * on).astype(o_ref.dtype)
        sf_scr[...] = state


def _retention(rqk, rvg, dec_f, dec_b, gn_w, batch, seq):
    t = rqk.shape[1]
    rows = _tile(seq, RET_BLOCK)
    chunk = _tile(rows, RET_CHUNK)
    nb = rows // chunk
    nblk = seq // rows

    def bidx(b, s):
        return b * nblk + jnp.where(s < nblk, nblk - 1 - s, s - nblk)

    def oidx(b, s):
        return b * nblk + jnp.maximum(s - nblk, 0)

    return pl.pallas_call(
        functools.partial(_ret_kernel, nblk=nblk, nb=nb, chunk=chunk),
        grid=(batch, RET_HEADS, 2 * nblk),
        in_specs=[
            pl.BlockSpec((None, 1, 1), lambda b, h, s: (h, 0, 0)),
            pl.BlockSpec((None, 1, 1), lambda b, h, s: (h, 0, 0)),
            pl.BlockSpec((None, rows, RET_QK_DIM), lambda b, h, s: (h, oidx(b, s), 0)),
            pl.BlockSpec((None, rows, RET_QK_DIM), lambda b, h, s: (RET_HEADS + h, bidx(b, s), 0)),
            pl.BlockSpec((None, rows, RET_V_DIM), lambda b, h, s: (h, bidx(b, s), 0)),
            pl.BlockSpec((None, rows, RET_V_DIM), lambda b, h, s: (RET_HEADS + h, oidx(b, s), 0)),
            pl.BlockSpec((None, 1, RET_V_DIM), lambda b, h, s: (h, 0, 0)),
        ],
        out_specs=pl.BlockSpec((rows, RET_V_DIM), lambda b, h, s: (oidx(b, s), h)),
        out_shape=jax.ShapeDtypeStruct((t, RV_COLS), BF16),
        scratch_shapes=[
            pltpu.VMEM((RET_QK_DIM, RET_V_DIM), F32),
            pltpu.VMEM((RET_QK_DIM, RET_V_DIM), F32),
            pltpu.VMEM((nblk * nb, RET_QK_DIM, RET_V_DIM), BF16),
        ],
        compiler_params=_params(("parallel", "parallel", "arbitrary")),
        name="retention",
    )(dec_f.reshape(RET_HEADS, 1, 1), dec_b.reshape(RET_HEADS, 1, 1),
      rqk, rqk, rvg, rvg, gn_w.reshape(RET_HEADS, 1, RET_V_DIM))


def _diff_kernel(lq1_ref, lk1_ref, lq2_ref, lk2_ref, sub_ref, q_ref, k_ref, vt_ref, o_ref,
                 acc_scr, s0_scr, s1_scr, m0_scr, m1_scr, *, n_kv, tk, tq, lambda_init):
    n_blocks = q_ref.shape[1] // tq
    s0 = (s0_scr, m0_scr)
    s1 = (s1_scr, m1_scr)
    bufs = (s0, s1)
    lam = (jnp.exp(jnp.sum(lq1_ref[...] * lk1_ref[...], axis=1, keepdims=True))
           - jnp.exp(jnp.sum(lq2_ref[...] * lk2_ref[...], axis=1, keepdims=True)) + lambda_init)

    def scores(r, j, buf):
        s_scr, m_scr = buf
        start = pl.multiple_of(j * tk, tk)
        for a in range(2):
            st = lax.dot_general(k_ref[a, pl.ds(start, tk), :],
                                 q_ref[a, r * tq:(r + 1) * tq, :], NT_DIMS,
                                 preferred_element_type=F32)
            s_scr[a] = st
            m_scr[a] = jnp.max(st, axis=0, keepdims=True)

    def absorb(j, buf, carry):
        s_scr, m_scr = buf
        vt = vt_ref[j]
        out = []
        for a in range(2):
            m_prev, l_prev = carry[a]
            st = s_scr[a]
            m_new = jnp.maximum(m_prev, m_scr[a])
            alpha = jnp.exp2(m_prev - m_new)
            p = jnp.exp2(st - m_new)
            l_new = alpha * l_prev + jnp.sum(p, axis=0, keepdims=True)
            acc_scr[a] = alpha * acc_scr[a] + jnp.dot(vt, p.astype(BF16),
                                                      preferred_element_type=F32)
            out.append((m_new, l_new))
        return tuple(out)

    def finalize(r, carry):
        (_, l0), (_, l1) = carry
        ot = acc_scr[0] * (1.0 / l0) - lam * (acc_scr[1] * (1.0 / l1))
        o = ot.T
        o = o * lax.rsqrt(jnp.mean(jnp.square(o), axis=-1, keepdims=True) + RMS_EPS)
        o_ref[r * tq:(r + 1) * tq, :] = (o * sub_ref[...] * (1.0 - lambda_init)).astype(o_ref.dtype)

    init = (jnp.full((1, tq), -jnp.inf, F32), jnp.zeros((1, tq), F32))
    unroll = next(u for u in (ATTN_KV_UNROLL, 2, 1) if n_kv % u == 0)

    def run(r, base, first, carry, last):
        for u in range(unroll):
            nxt = bufs[(first + u + 1) % 2]
            if not last or u + 1 < unroll:
                scores(r, base + u + 1, nxt)
            elif r + 1 < n_blocks:
                scores(r + 1, 0, nxt)
            carry = absorb(base + u, bufs[(first + u) % 2], carry)
        return carry

    scores(0, 0, s0)
    for r in range(n_blocks):
        first = (r * n_kv) % 2
        acc_scr[...] = jnp.zeros_like(acc_scr)
        carry = lax.fori_loop(
            0, n_kv // unroll - 1,
            lambda i, c, r=r, first=first: run(r, i * unroll, first, c, False), (init, init))
        carry = run(r, n_kv - unroll, first, carry, True)
        finalize(r, carry)


def _diff_attention(dqk, dvt, lq1, lk1, lq2, lk2, subln_w, batch, seq, lambda_init):
    t = dqk.shape[1]
    tk = dvt.shape[3]
    n_kv = seq // tk
    assert n_kv == 1 or n_kv % 2 == 0, n_kv
    tq = _tile(seq, ATTN_Q_ROWS)
    rows = _tile(seq, ATTN_Q_ROWS * ATTN_Q_BLOCKS)
    nq = seq // rows
    vec = lambda a: a.reshape(1, -1).astype(F32)
    vspec = lambda n: pl.BlockSpec((1, n), lambda b, h, i: (0, 0))
    return pl.pallas_call(
        functools.partial(_diff_kernel, n_kv=n_kv, tk=tk, tq=tq, lambda_init=lambda_init),
        grid=(batch, DIFF_HEADS, nq),
        in_specs=[
            vspec(DIFF_QK_DIM), vspec(DIFF_QK_DIM), vspec(DIFF_QK_DIM), vspec(DIFF_QK_DIM),
            vspec(DIFF_V_DIM),
            pl.BlockSpec((2, rows, DIFF_QK_DIM), lambda b, h, i: (h, b * nq + i, 0)),
            pl.BlockSpec((2, seq, DIFF_QK_DIM), lambda b, h, i: (DIFF_HEADS + h, b, 0)),
            pl.BlockSpec((None, n_kv, DIFF_V_DIM, tk), lambda b, h, i: (h, b, 0, 0)),
        ],
        out_specs=pl.BlockSpec((rows, DIFF_V_DIM), lambda b, h, i: (b * nq + i, h)),
        out_shape=jax.ShapeDtypeStruct((t, DV_COLS), BF16),
        scratch_shapes=[pltpu.VMEM((2, DIFF_V_DIM, tq), F32),
                        pltpu.VMEM((2, tk, tq), F32),
                        pltpu.VMEM((2, tk, tq), F32),
                        pltpu.VMEM((2, 1, tq), F32),
                        pltpu.VMEM((2, 1, tq), F32)],
        compiler_params=_params(("parallel", "parallel", "arbitrary")),
        name="diff_attention",
    )(vec(lq1), vec(lk1), vec(lq2), vec(lk2), vec(subln_w), dqk, dqk, dvt)


def _outproj_kernel(ro_ref, do_ref, w_ref, x_ref, g_ref, b_ref, o32_ref, o16_ref,
                    mix_scr, pre_scr, *, ni, nj):
    i = pl.program_id(0)
    j = pl.program_id(1)
    half = ro_ref.shape[1]
    rows = o32_ref.shape[0]

    @pl.when(jnp.logical_and(j == 0, i < ni))
    def _():
        mix_scr[:, :half] = ro_ref[...]
        mix_scr[:, half:] = do_ref[...]

    def produce():
        pre_scr[i % 2, j] = jnp.dot(mix_scr[...], w_ref[j], preferred_element_type=F32)

    def normalise():
        slot = (i + 1) % 2
        rs = pl.ds(pl.multiple_of(j * rows, rows), rows)
        pre = jnp.concatenate([pre_scr[slot, c, rs, :] for c in range(nj)], axis=1)
        pre = pre + DEEPNORM_ALPHA * x_ref[...]
        y = _layer_norm(pre, g_ref[...], b_ref[...])
        o32_ref[...] = y
        o16_ref[...] = y.astype(o16_ref.dtype)

    def both():
        normalise()
        produce()

    _three_phase(i, ni, produce, both, normalise)


def _out_projection(ro, do, w_out, x, ln_g, ln_b):
    t = x.shape[0]
    tm = _tile(t, OUT_ROWS)
    nj, _, tn = w_out.shape
    ni = t // tm
    rows = tm // nj
    half = RV_COLS
    tile = lambda i: jnp.minimum(i, ni - 1)
    prev = lambda i, j: (jnp.where(i == 0, 0, (i - 1) * nj + j), 0)
    return pl.pallas_call(
        functools.partial(_outproj_kernel, ni=ni, nj=nj),
        grid=(ni + 1, nj),
        in_specs=[
            pl.BlockSpec((tm, half), lambda i, j: (tile(i), 0)),
            pl.BlockSpec((tm, half), lambda i, j: (tile(i), 0)),
            pl.BlockSpec((nj, 2 * half, tn), lambda i, j: (0, 0, 0)),
            pl.BlockSpec((rows, D_MODEL), prev),
            pl.BlockSpec((1, D_MODEL), lambda i, j: (0, 0)),
            pl.BlockSpec((1, D_MODEL), lambda i, j: (0, 0)),
        ],
        out_specs=[pl.BlockSpec((rows, D_MODEL), prev), pl.BlockSpec((rows, D_MODEL), prev)],
        out_shape=[jax.ShapeDtypeStruct((t, D_MODEL), F32),
                   jax.ShapeDtypeStruct((t, D_MODEL), BF16)],
        scratch_shapes=[pltpu.VMEM((tm, 2 * half), BF16),
                        pltpu.VMEM((2, nj, tm, tn), F32)],
        compiler_params=_params(("arbitrary", "arbitrary")),
        name="out_projection",
    )(ro, do, w_out, x, ln_g.reshape(1, -1), ln_b.reshape(1, -1))


def _gate_up_kernel(x_ref, wg_ref, wu_ref, o_ref):
    x = x_ref[...]
    g = jnp.dot(x, wg_ref[...], preferred_element_type=F32)
    u = jnp.dot(x, wu_ref[...], preferred_element_type=F32)
    o_ref[...] = (_swish(g) * u).astype(o_ref.dtype)


def _gate_up(xb, wg, wu):
    t, d = xb.shape
    n = wg.shape[1]
    tm = _tile(t, GATE_UP_ROWS)
    tn = GATE_UP_COLS
    return pl.pallas_call(
        _gate_up_kernel,
        grid=(t // tm, n // tn),
        in_specs=[pl.BlockSpec((tm, d), lambda i, j: (i, 0)),
                  pl.BlockSpec((d, tn), lambda i, j: (0, j)),
                  pl.BlockSpec((d, tn), lambda i, j: (0, j))],
        out_specs=pl.BlockSpec((tm, tn), lambda i, j: (i, j)),
        out_shape=jax.ShapeDtypeStruct((t, n), BF16),
        compiler_params=_params(("parallel", "arbitrary")),
        name="ffn_gate_up",
    )(xb, wg, wu)


def _down_kernel(h_ref, w_ref, x_ref, g_ref, b_ref, o_ref, acc_scr, *, ni, n_ln):
    i = pl.program_id(0)
    k = pl.program_id(1)
    rows = o_ref.shape[0]
    width = o_ref.shape[1]

    def produce():
        h = h_ref[...]
        slot = i % 2
        for c in range(width // DOWN_COL_CHUNK):
            sl = slice(c * DOWN_COL_CHUNK, (c + 1) * DOWN_COL_CHUNK)
            part = jnp.dot(h, w_ref[:, sl], preferred_element_type=F32)
            acc_scr[slot, :, sl] = jnp.where(k > 0, acc_scr[slot, :, sl], 0.0) + part

    def normalise():
        slot = (i + 1) % 2
        chunk = jnp.minimum(k, n_ln - 1)
        rs = pl.ds(pl.multiple_of(chunk * rows, rows), rows)
        pre = DEEPNORM_ALPHA * x_ref[...] + acc_scr[slot, rs, :]
        o_ref[...] = _layer_norm(pre, g_ref[...], b_ref[...])

    def first():
        @pl.when(k == 0)
        def _():
            acc_scr[...] = jnp.zeros_like(acc_scr)

        produce()

    def both():
        normalise()
        produce()

    _three_phase(i, ni, first, both, normalise)


def _down_projection(h, wd, x1, ln_g, ln_b):
    t, kdim = h.shape
    tm = _tile(t, DOWN_ROWS)
    tk = DOWN_K
    ni = t // tm
    nk = kdim // tk
    rows = LN_ROW_CHUNK
    n_ln = tm // rows
    assert n_ln <= nk, (n_ln, nk)
    tile = lambda i: jnp.minimum(i, ni - 1)
    prev = lambda i, k: (jnp.where(i == 0, 0, (i - 1) * n_ln + jnp.minimum(k, n_ln - 1)), 0)
    return pl.pallas_call(
        functools.partial(_down_kernel, ni=ni, n_ln=n_ln),
        grid=(ni + 1, nk),
        in_specs=[pl.BlockSpec((tm, tk), lambda i, k: (tile(i), k)),
                  pl.BlockSpec((tk, D_MODEL), lambda i, k: (k, 0)),
                  pl.BlockSpec((rows, D_MODEL), prev),
                  pl.BlockSpec((1, D_MODEL), lambda i, k: (0, 0)),
                  pl.BlockSpec((1, D_MODEL), lambda i, k: (0, 0))],
        out_specs=pl.BlockSpec((rows, D_MODEL), prev),
        out_shape=jax.ShapeDtypeStruct((t, D_MODEL), F32),
        scratch_shapes=[pltpu.VMEM((2, tm, D_MODEL), F32)],
        compiler_params=_params(("arbitrary", "arbitrary")),
        name="ffn_down",
    )(h, wd, x1, ln_g.reshape(1, -1), ln_b.reshape(1, -1))


def _rope_tables(seq):
    pos = jnp.arange(seq, dtype=F32)

    def cos_sin(rot_dim, base):
        inv_freq = base ** (-jnp.arange(0, rot_dim, 2, dtype=F32) / rot_dim)
        ang = pos[:, None] * inv_freq[None, :]
        return jnp.cos(ang), jnp.sin(ang)

    c, s = cos_sin(RET_QK_DIM, RET_ROT_BASE)
    ret_tab = jnp.concatenate([c, c, -s, s], axis=-1)

    c, s = cos_sin(DIFF_ROT_DIM, ROPE_THETA)
    half = DIFF_ROT_DIM // 2
    rest = DIFF_QK_DIM - DIFF_ROT_DIM
    ones = jnp.ones((seq, rest), F32)
    zeros = jnp.zeros((seq, rest), F32)
    zh = jnp.zeros((seq, half), F32)
    diff_tab = jnp.concatenate([c, c, ones, zh, s, zeros, -s, zh, zeros], axis=-1)
    return ret_tab, diff_tab


def _prepare_weights(w_in, w_out, w_gate, w_up, w_down):
    wb = w_in.astype(BF16)
    c0 = 2 * RQ_COLS
    c1 = c0 + 2 * RV_COLS
    c2 = c1 + 2 * DQ_COLS
    pad = D_FF_PAD - D_FF
    zc = jnp.zeros((w_gate.shape[0], pad), BF16)
    wg = jnp.concatenate([w_gate.astype(BF16), zc], axis=1)
    wu = jnp.concatenate([w_up.astype(BF16), zc], axis=1)
    wd = jnp.concatenate([w_down.astype(BF16), jnp.zeros((pad, w_down.shape[1]), BF16)], axis=0)
    k, n = w_out.shape
    wo = w_out.astype(BF16).reshape(k, OUT_COL_GROUPS, n // OUT_COL_GROUPS).transpose(1, 0, 2)
    return wb[:, :c0], wb[:, c0:c1], wb[:, c1:c2], wb[:, c2:], wo, wg, wu, wd


def _encoder_layer(x, weights, tables, dec_f, dec_b, gn_w, lq1, lk1, lq2, lk2, subln_w,
                   ln1_g, ln1_b, ln2_g, ln2_b, lambda_init):
    batch, seq, d = x.shape
    w_rqk, w_rvg, w_dqk, w_dv, w_out, wg, wu, wd = weights
    x2 = x.reshape(batch * seq, d)
    ret_tab, diff_tab = tables

    xb = x2.astype(BF16)
    rvg = _project(xb, w_rvg, seq, "plain", RET_V_DIM)
    rqk = _project(xb, w_rqk, seq, "rope_full", RET_QK_DIM, ret_tab,
                   (1.0, RET_QK_DIM ** -0.5), RQ_COLS)
    dqk = _project(xb, w_dqk, seq, "rope_partial", DIFF_QK_DIM, diff_tab,
                   (DIFF_QK_DIM ** -0.5 * LOG2_E, 1.0), DQ_COLS)
    dvt = _project(xb, w_dv, seq, "plain_t", DIFF_V_DIM)

    ro = _retention(rqk, rvg, dec_f, dec_b, gn_w, batch, seq)
    do = _diff_attention(dqk, dvt, lq1, lk1, lq2, lk2, subln_w, batch, seq, lambda_init)

    x1, x1b = _out_projection(ro, do, w_out, x2, ln1_g, ln1_b)
    h = _gate_up(x1b, wg, wu)
    y = _down_projection(h, wd, x1, ln2_g, ln2_b)
    return y.reshape(batch, seq, d)


def kernel(x_prompt, x_sample, w_in, ret_decay_f, ret_decay_b, ret_gn_w, diff_lambda_q1,
           diff_lambda_k1, diff_lambda_q2, diff_lambda_k2, diff_subln_w, w_out, ln1_g, ln1_b,
           w_gate, w_up, w_down, ln2_g, ln2_b):
    y_prompt, y_sample = x_prompt, x_sample
    for l in range(DEPTH):
        lambda_init = 0.8 - 0.6 * math.exp(-0.3 * l)
        weights = _prepare_weights(w_in[l], w_out[l], w_gate[l], w_up[l], w_down[l])
        tables = _rope_tables(max(y_prompt.shape[1], y_sample.shape[1]))
        rest = (tables, ret_decay_f[l], ret_decay_b[l], ret_gn_w[l], diff_lambda_q1[l], diff_lambda_k1[l],
                diff_lambda_q2[l], diff_lambda_k2[l], diff_subln_w[l], ln1_g[l], ln1_b[l],
                ln2_g[l], ln2_b[l], lambda_init)
        y_prompt = _encoder_layer(y_prompt, weights, *rest)
        y_sample = _encoder_layer(y_sample, weights, *rest)
    return (y_prompt, y_sample)
```

```python
import functools
import math

import jax
import jax.numpy as jnp
from jax import lax
from jax.experimental import pallas as pl
from jax.experimental.pallas import tpu as pltpu

D_MODEL = 4096
DEPTH = 1
RET_HEADS = 8
RET_QK_DIM = 128
RET_V_DIM = 256
RET_ROT_BASE = 10000.0
DIFF_HEADS = 8
DIFF_QK_DIM = 128
DIFF_V_DIM = 256
DIFF_ROT_DIM = DIFF_QK_DIM // 4
ROPE_THETA = 500000.0
D_FF = 11008
DEEPNORM_ALPHA = (2.0 * DEPTH) ** 0.25
LN_EPS = 1e-5
GN_EPS = 1e-6
RMS_EPS = 1e-5

RQ_COLS = RET_HEADS * RET_QK_DIM
RV_COLS = RET_HEADS * RET_V_DIM
DQ_COLS = DIFF_HEADS * 2 * DIFF_QK_DIM
DV_COLS = DIFF_HEADS * DIFF_V_DIM

LANES = 128
VMEM_LIMIT_BYTES = 56 * 1024 * 1024
D_FF_PAD = 11264
PROJ_ROWS = 1024
PROJ_COLS = 512
PROJ_ROW_SPLITS = {"plain": 1, "plain_t": 4, "rope_full": 4, "rope_partial": 4}
OUT_ROWS = 256
OUT_COL_GROUPS = 4
GATE_UP_ROWS = 1024
GATE_UP_COLS = 512
DOWN_ROWS = 1024
DOWN_K = 512
RET_CHUNK = 256
RET_BLOCK = 4096
ATTN_Q_ROWS = 512
ATTN_Q_BLOCKS = 2
ATTN_KV_UNROLL = 4
DOWN_COL_CHUNK = 1024
LN_ROW_CHUNK = 64
LOG2_E = math.log2(math.e)

BF16 = jnp.bfloat16
F32 = jnp.float32

NT_DIMS = (((1,), (1,)), ((), ()))
TN_DIMS = (((0,), (0,)), ((), ()))


def _tile(n, pref):
    t = pref
    while t > 8 and n % t:
        t //= 2
    assert n % t == 0, (n, pref)
    return t


def _params(sem):
    return pltpu.CompilerParams(dimension_semantics=sem, vmem_limit_bytes=VMEM_LIMIT_BYTES)


def _swish(g):
    return g * (1.0 / (1.0 + jnp.exp(-g)))


def _layer_norm(pre, g, b):
    mu = jnp.mean(pre, axis=-1, keepdims=True)
    var = jnp.mean(jnp.square(pre - mu), axis=-1, keepdims=True)
    return (pre - mu) * lax.rsqrt(var + LN_EPS) * g + b


def _three_phase(i, n, first, middle, last):
    pl.when(i == 0)(first)
    pl.when(jnp.logical_and(i > 0, i < n))(middle)
    pl.when(i == n)(last)


def _proj_kernel(x_ref, w_ref, *rest, mode, hw, scales, split):
    if mode in ("rope_full", "rope_partial"):
        tab_ref, o_ref = rest
        scale = jnp.where(pl.program_id(1) < split, scales[0], scales[1]).astype(F32)
    else:
        (o_ref,) = rest
    tm = x_ref.shape[0]
    n_split = PROJ_ROW_SPLITS[mode]
    rows = tm // n_split
    for r in range(n_split):
        rs = slice(r * rows, (r + 1) * rows)
        acc = jnp.dot(x_ref[rs, :], w_ref[...], preferred_element_type=F32)
        for c in range(o_ref.shape[0]):
            blk = acc[:, c * hw:(c + 1) * hw]
            if mode == "rope_full":
                cos = tab_ref[rs, 0:LANES]
                sin = tab_ref[rs, LANES:2 * LANES]
                blk = (blk * cos + pltpu.roll(blk, LANES // 2, 1) * sin) * scale
            elif mode == "rope_partial":
                half = DIFF_ROT_DIM // 2
                a = tab_ref[rs, 0:LANES]
                b = tab_ref[rs, LANES:2 * LANES]
                c2 = tab_ref[rs, 2 * LANES:3 * LANES]
                blk = (blk * a + pltpu.roll(blk, half, 1) * b
                       + pltpu.roll(blk, LANES - half, 1) * c2) * scale
            if mode == "plain_t":
                o_ref[c, :, rs] = blk.T.astype(o_ref.dtype)
            else:
                o_ref[c, rs, :] = blk.astype(o_ref.dtype)


def _project(x, w, seq, mode, hw, tab=None, scales=None, split_cols=None):
    t, d = x.shape
    n = w.shape[1]
    tm = _tile(seq, PROJ_ROWS)
    tn = _tile(n, PROJ_COLS)
    grid = (t // tm, n // tn)
    in_specs = [pl.BlockSpec((tm, d), lambda i, j: (i, 0)),
                pl.BlockSpec((d, tn), lambda i, j: (0, j))]
    args = [x, w]
    if tab is not None:
        per_seq = seq // tm
        in_specs.append(pl.BlockSpec((tm, tab.shape[1]), lambda i, j: (i % per_seq, 0)))
        args.append(tab)
    if mode == "plain_t":
        out_spec = pl.BlockSpec((tn // hw, None, hw, tm), lambda i, j: (j, i, 0, 0))
        out_shape = jax.ShapeDtypeStruct((n // hw, t // tm, hw, tm), BF16)
    else:
        out_spec = pl.BlockSpec((tn // hw, tm, hw), lambda i, j: (j, i, 0))
        out_shape = jax.ShapeDtypeStruct((n // hw, t, hw), BF16)
    return pl.pallas_call(
        functools.partial(_proj_kernel, mode=mode, hw=hw, scales=scales,
                          split=None if split_cols is None else split_cols // tn),
        grid=grid,
        in_specs=in_specs,
        out_specs=out_spec,
        out_shape=out_shape,
        compiler_params=_params(("parallel", "arbitrary")),
        name="proj_" + mode,
    )(*args)


def _ret_kernel(decf_ref, decb_ref, q_ref, k_ref, v_ref, g_ref, gnw_ref, o_ref,
                sf_scr, sb_scr, sball_scr, *, nblk, nb, chunk):
    s = pl.program_id(2)
    lg_f = -jnp.exp(decf_ref[...])
    lg_b = -jnp.exp(decb_ref[...])
    row = lax.broadcasted_iota(jnp.int32, (chunk, 1), 0).astype(F32)

    @pl.when(s == 0)
    def _():
        sb_scr[...] = jnp.zeros_like(sb_scr)

    @pl.when(s < nblk)
    def _():
        blk = nblk - 1 - s
        zeta_b = jnp.exp(row * lg_b)
        decay_b = jnp.exp(chunk * lg_b)
        state = sb_scr[...]
        for ci in reversed(range(nb)):
            rs = slice(ci * chunk, (ci + 1) * chunk)
            sball_scr[blk * nb + ci] = state.astype(BF16)
            kz = (k_ref[rs, :].astype(F32) * zeta_b).astype(BF16)
            state = decay_b * state + lax.dot_general(kz, v_ref[rs, :], TN_DIMS,
                                                      preferred_element_type=F32)
        sb_scr[...] = state

    @pl.when(s == nblk)
    def _():
        sf_scr[...] = jnp.zeros_like(sf_scr)

    @pl.when(s >= nblk)
    def _():
        blk = s - nblk
        rel = (lax.broadcasted_iota(jnp.int32, (chunk, chunk), 0)
               - lax.broadcasted_iota(jnp.int32, (chunk, chunk), 1)).astype(F32)
        dmat = (jnp.where(rel >= 0, jnp.exp(jnp.maximum(rel, 0.0) * lg_f), 0.0)
                + jnp.where(rel <= 0, jnp.exp(jnp.maximum(-rel, 0.0) * lg_b), 0.0))
        xi_f = jnp.exp((row + 1.0) * lg_f)
        xi_b = jnp.exp((chunk - row) * lg_b)
        zeta_f = jnp.exp((chunk - 1.0 - row) * lg_f)
        decay_f = jnp.exp(chunk * lg_f)
        gnw = gnw_ref[...]
        state = sf_scr[...]
        for ci in range(nb):
            rs = slice(ci * chunk, (ci + 1) * chunk)
            q = q_ref[rs, :]
            k = k_ref[rs, :]
            v = v_ref[rs, :]
            sc = lax.dot_general(q, k, NT_DIMS, preferred_element_type=F32)
            qf = q.astype(F32)
            o = jnp.dot((sc * dmat).astype(BF16), v, preferred_element_type=F32)
            o += jnp.dot((qf * xi_f).astype(BF16), state.astype(BF16), preferred_element_type=F32)
            o += jnp.dot((qf * xi_b).astype(BF16), sball_scr[blk * nb + ci],
                         preferred_element_type=F32)
            kz = (k.astype(F32) * zeta_f).astype(BF16)
            state = decay_f * state + lax.dot_general(kz, v, TN_DIMS, preferred_element_type=F32)
            mu = jnp.mean(o, axis=-1, keepdims=True)
            var = jnp.mean(jnp.square(o - mu), axis=-1, keepdims=True)
            on = (o - mu) * lax.rsqrt(var + GN_EPS) * gnw
            o_ref[rs, :] = (_swish(g_ref[rs, :].astype(F32)) * on).astype(o_ref.dtype)
        sf_scr[...] = state


def _retention(rqk, rvg, dec_f, dec_b, gn_w, batch, seq):
    t = rqk.shape[1]
    rows = _tile(seq, RET_BLOCK)
    chunk = _tile(rows, RET_CHUNK)
    nb = rows // chunk
    nblk = seq // rows

    def bidx(b, s):
        return b * nblk + jnp.where(s < nblk, nblk - 1 - s, s - nblk)

    def oidx(b, s):
        return b * nblk + jnp.maximum(s - nblk, 0)

    return pl.pallas_call(
        functools.partial(_ret_kernel, nblk=nblk, nb=nb, chunk=chunk),
        grid=(batch, RET_HEADS, 2 * nblk),
        in_specs=[
            pl.BlockSpec((None, 1, 1), lambda b, h, s: (h, 0, 0)),
            pl.BlockSpec((None, 1, 1), lambda b, h, s: (h, 0, 0)),
            pl.BlockSpec((None, rows, RET_QK_DIM), lambda b, h, s: (h, oidx(b, s), 0)),
            pl.BlockSpec((None, rows, RET_QK_DIM), lambda b, h, s: (RET_HEADS + h, bidx(b, s), 0)),
            pl.BlockSpec((None, rows, RET_V_DIM), lambda b, h, s: (h, bidx(b, s), 0)),
            pl.BlockSpec((None, rows, RET_V_DIM), lambda b, h, s: (RET_HEADS + h, oidx(b, s), 0)),
            pl.BlockSpec((None, 1, RET_V_DIM), lambda b, h, s: (h, 0, 0)),
        ],
        out_specs=pl.BlockSpec((rows, RET_V_DIM), lambda b, h, s: (oidx(b, s), h)),
        out_shape=jax.ShapeDtypeStruct((t, RV_COLS), BF16),
        scratch_shapes=[
            pltpu.VMEM((RET_QK_DIM, RET_V_DIM), F32),
            pltpu.VMEM((RET_QK_DIM, RET_V_DIM), F32),
            pltpu.VMEM((nblk * nb, RET_QK_DIM, RET_V_DIM), BF16),
        ],
        compiler_params=_params(("parallel", "parallel", "arbitrary")),
        name="retention",
    )(dec_f.reshape(RET_HEADS, 1, 1), dec_b.reshape(RET_HEADS, 1, 1),
      rqk, rqk, rvg, rvg, gn_w.reshape(RET_HEADS, 1, RET_V_DIM))


def _diff_kernel(lq1_ref, lk1_ref, lq2_ref, lk2_ref, sub_ref, q_ref, k_ref, vt_ref, o_ref,
                 acc_scr, s0_scr, s1_scr, m0_scr, m1_scr, *, n_kv, tk, tq, lambda_init):
    n_blocks = q_ref.shape[1] // tq
    s0 = (s0_scr, m0_scr)
    s1 = (s1_scr, m1_scr)
    bufs = (s0, s1)
    lam = (jnp.exp(jnp.sum(lq1_ref[...] * lk1_ref[...], axis=1, keepdims=True))
           - jnp.exp(jnp.sum(lq2_ref[...] * lk2_ref[...], axis=1, keepdims=True)) + lambda_init)

    def scores(r, j, buf):
        s_scr, m_scr = buf
        start = pl.multiple_of(j * tk, tk)
        for a in range(2):
            st = lax.dot_general(k_ref[a, pl.ds(start, tk), :],
                                 q_ref[a, r * tq:(r + 1) * tq, :], NT_DIMS,
                                 preferred_element_type=F32)
            s_scr[a] = st
            m_scr[a] = jnp.max(st, axis=0, keepdims=True)

    def absorb(j, buf, carry):
        s_scr, m_scr = buf
        vt = vt_ref[j]
        out = []
        for a in range(2):
            m_prev, l_prev = carry[a]
            st = s_scr[a]
            m_new = jnp.maximum(m_prev, m_scr[a])
            alpha = jnp.exp2(m_prev - m_new)
            p = jnp.exp2(st - m_new)
            l_new = alpha * l_prev + jnp.sum(p, axis=0, keepdims=True)
            acc_scr[a] = alpha * acc_scr[a] + jnp.dot(vt, p.astype(BF16),
                                                      preferred_element_type=F32)
            out.append((m_new, l_new))
        return tuple(out)

    def finalize(r, carry):
        (_, l0), (_, l1) = carry
        ot = acc_scr[0] * (1.0 / l0) - lam * (acc_scr[1] * (1.0 / l1))
        o = ot.T
        o = o * lax.rsqrt(jnp.mean(jnp.square(o), axis=-1, keepdims=True) + RMS_EPS)
        o_ref[r * tq:(r + 1) * tq, :] = (o * sub_ref[...] * (1.0 - lambda_init)).astype(o_ref.dtype)

    init = (jnp.full((1, tq), -jnp.inf, F32), jnp.zeros((1, tq), F32))
    unroll = next(u for u in (ATTN_KV_UNROLL, 2, 1) if n_kv % u == 0)

    def run(r, base, first, carry, last):
        for u in range(unroll):
            nxt = bufs[(first + u + 1) % 2]
            if not last or u + 1 < unroll:
                scores(r, base + u + 1, nxt)
            elif r + 1 < n_blocks:
                scores(r + 1, 0, nxt)
            carry = absorb(base + u, bufs[(first + u) % 2], carry)
        return carry

    scores(0, 0, s0)
    for r in range(n_blocks):
        first = (r * n_kv) % 2
        acc_scr[...] = jnp.zeros_like(acc_scr)
        carry = lax.fori_loop(
            0, n_kv // unroll - 1,
            lambda i, c, r=r, first=first: run(r, i * unroll, first, c, False), (init, init))
        carry = run(r, n_kv - unroll, first, carry, True)
        finalize(r, carry)


def _diff_attention(dqk, dvt, lq1, lk1, lq2, lk2, subln_w, batch, seq, lambda_init):
    t = dqk.shape[1]
    tk = dvt.shape[3]
    n_kv = seq // tk
    assert n_kv == 1 or n_kv % 2 == 0, n_kv
    tq = _tile(seq, ATTN_Q_ROWS)
    rows = _tile(seq, ATTN_Q_ROWS * ATTN_Q_BLOCKS)
    nq = seq // rows
    vec = lambda a: a.reshape(1, -1).astype(F32)
    vspec = lambda n: pl.BlockSpec((1, n), lambda b, h, i: (0, 0))
    return pl.pallas_call(
        functools.partial(_diff_kernel, n_kv=n_kv, tk=tk, tq=tq, lambda_init=lambda_init),
        grid=(batch, DIFF_HEADS, nq),
        in_specs=[
            vspec(DIFF_QK_DIM), vspec(DIFF_QK_DIM), vspec(DIFF_QK_DIM), vspec(DIFF_QK_DIM),
            vspec(DIFF_V_DIM),
            pl.BlockSpec((2, rows, DIFF_QK_DIM), lambda b, h, i: (h, b * nq + i, 0)),
            pl.BlockSpec((2, seq, DIFF_QK_DIM), lambda b, h, i: (DIFF_HEADS + h, b, 0)),
            pl.BlockSpec((None, n_kv, DIFF_V_DIM, tk), lambda b, h, i: (h, b, 0, 0)),
        ],
        out_specs=pl.BlockSpec((rows, DIFF_V_DIM), lambda b, h, i: (b * nq + i, h)),
        out_shape=jax.ShapeDtypeStruct((t, DV_COLS), BF16),
        scratch_shapes=[pltpu.VMEM((2, DIFF_V_DIM, tq), F32),
                        pltpu.VMEM((2, tk, tq), F32),
                        pltpu.VMEM((2, tk, tq), F32),
                        pltpu.VMEM((2, 1, tq), F32),
                        pltpu.VMEM((2, 1, tq), F32)],
        compiler_params=_params(("parallel", "parallel", "arbitrary")),
        name="diff_attention",
    )(vec(lq1), vec(lk1), vec(lq2), vec(lk2), vec(subln_w), dqk, dqk, dvt)


def _outproj_kernel(ro_ref, do_ref, w_ref, x_ref, g_ref, b_ref, o32_ref, o16_ref,
                    mix_scr, pre_scr, *, ni, nj):
    i = pl.program_id(0)
    j = pl.program_id(1)
    half = ro_ref.shape[1]
    rows = o32_ref.shape[0]

    @pl.when(jnp.logical_and(j == 0, i < ni))
    def _():
        mix_scr[:, :half] = ro_ref[...]
        mix_scr[:, half:] = do_ref[...]

    def produce():
        pre_scr[i % 2, j] = jnp.dot(mix_scr[...], w_ref[j], preferred_element_type=F32)

    def normalise():
        slot = (i + 1) % 2
        rs = pl.ds(pl.multiple_of(j * rows, rows), rows)
        pre = jnp.concatenate([pre_scr[slot, c, rs, :] for c in range(nj)], axis=1)
        pre = pre + DEEPNORM_ALPHA * x_ref[...]
        y = _layer_norm(pre, g_ref[...], b_ref[...])
        o32_ref[...] = y
        o16_ref[...] = y.astype(o16_ref.dtype)

    def both():
        normalise()
        produce()

    _three_phase(i, ni, produce, both, normalise)


def _out_projection(ro, do, w_out, x, ln_g, ln_b):
    t = x.shape[0]
    tm = _tile(t, OUT_ROWS)
    nj, _, tn = w_out.shape
    ni = t // tm
    rows = tm // nj
    half = RV_COLS
    tile = lambda i: jnp.minimum(i, ni - 1)
    prev = lambda i, j: (jnp.where(i == 0, 0, (i - 1) * nj + j), 0)
    return pl.pallas_call(
        functools.partial(_outproj_kernel, ni=ni, nj=nj),
        grid=(ni + 1, nj),
        in_specs=[
            pl.BlockSpec((tm, half), lambda i, j: (tile(i), 0)),
            pl.BlockSpec((tm, half), lambda i, j: (tile(i), 0)),
            pl.BlockSpec((nj, 2 * half, tn), lambda i, j: (0, 0, 0)),
            pl.BlockSpec((rows, D_MODEL), prev),
            pl.BlockSpec((1, D_MODEL), lambda i, j: (0, 0)),
            pl.BlockSpec((1, D_MODEL), lambda i, j: (0, 0)),
        ],
        out_specs=[pl.BlockSpec((rows, D_MODEL), prev), pl.BlockSpec((rows, D_MODEL), prev)],
        out_shape=[jax.ShapeDtypeStruct((t, D_MODEL), F32),
                   jax.ShapeDtypeStruct((t, D_MODEL), BF16)],
        scratch_shapes=[pltpu.VMEM((tm, 2 * half), BF16),
                        pltpu.VMEM((2, nj, tm, tn), F32)],
        compiler_params=_params(("arbitrary", "arbitrary")),
        name="out_projection",
    )(ro, do, w_out, x, ln_g.reshape(1, -1), ln_b.reshape(1, -1))


def _gate_up_kernel(x_ref, wg_ref, wu_ref, o_ref):
    x = x_ref[...]
    g = jnp.dot(x, wg_ref[...], preferred_element_type=F32)
    u = jnp.dot(x, wu_ref[...], preferred_element_type=F32)
    o_ref[...] = (_swish(g) * u).astype(o_ref.dtype)


def _gate_up(xb, wg, wu):
    t, d = xb.shape
    n = wg.shape[1]
    tm = _tile(t, GATE_UP_ROWS)
    tn = GATE_UP_COLS
    return pl.pallas_call(
        _gate_up_kernel,
        grid=(t // tm, n // tn),
        in_specs=[pl.BlockSpec((tm, d), lambda i, j: (i, 0)),
                  pl.BlockSpec((d, tn), lambda i, j: (0, j)),
                  pl.BlockSpec((d, tn), lambda i, j: (0, j))],
        out_specs=pl.BlockSpec((tm, tn), lambda i, j: (i, j)),
        out_shape=jax.ShapeDtypeStruct((t, n), BF16),
        compiler_params=_params(("parallel", "arbitrary")),
        name="ffn_gate_up",
    )(xb, wg, wu)


def _down_kernel(h_ref, w_ref, x_ref, g_ref, b_ref, o_ref, acc_scr, *, ni, n_ln):
    i = pl.program_id(0)
    k = pl.program_id(1)
    rows = o_ref.shape[0]
    width = o_ref.shape[1]

    def produce():
        h = h_ref[...]
        slot = i % 2
        for c in range(width // DOWN_COL_CHUNK):
            sl = slice(c * DOWN_COL_CHUNK, (c + 1) * DOWN_COL_CHUNK)
            part = jnp.dot(h, w_ref[:, sl], preferred_element_type=F32)
            acc_scr[slot, :, sl] = jnp.where(k > 0, acc_scr[slot, :, sl], 0.0) + part

    def normalise():
        slot = (i + 1) % 2
        chunk = jnp.minimum(k, n_ln - 1)
        rs = pl.ds(pl.multiple_of(chunk * rows, rows), rows)
        pre = DEEPNORM_ALPHA * x_ref[...] + acc_scr[slot, rs, :]
        o_ref[...] = _layer_norm(pre, g_ref[...], b_ref[...])

    def first():
        @pl.when(k == 0)
        def _():
            acc_scr[...] = jnp.zeros_like(acc_scr)

        produce()

    def both():
        normalise()
        produce()

    _three_phase(i, ni, first, both, normalise)


def _down_projection(h, wd, x1, ln_g, ln_b):
    t, kdim = h.shape
    tm = _tile(t, DOWN_ROWS)
    tk = DOWN_K
    ni = t // tm
    nk = kdim // tk
    rows = LN_ROW_CHUNK
    n_ln = tm // rows
    assert n_ln <= nk, (n_ln, nk)
    tile = lambda i: jnp.minimum(i, ni - 1)
    prev = lambda i, k: (jnp.where(i == 0, 0, (i - 1) * n_ln + jnp.minimum(k, n_ln - 1)), 0)
    return pl.pallas_call(
        functools.partial(_down_kernel, ni=ni, n_ln=n_ln),
        grid=(ni + 1, nk),
        in_specs=[pl.BlockSpec((tm, tk), lambda i, k: (tile(i), k)),
                  pl.BlockSpec((tk, D_MODEL), lambda i, k: (k, 0)),
                  pl.BlockSpec((rows, D_MODEL), prev),
                  pl.BlockSpec((1, D_MODEL), lambda i, k: (0, 0)),
                  pl.BlockSpec((1, D_MODEL), lambda i, k: (0, 0))],
        out_specs=pl.BlockSpec((rows, D_MODEL), prev),
        out_shape=jax.ShapeDtypeStruct((t, D_MODEL), F32),
        scratch_shapes=[pltpu.VMEM((2, tm, D_MODEL), F32)],
        compiler_params=_params(("arbitrary", "arbitrary")),
        name="ffn_down",
    )(h, wd, x1, ln_g.reshape(1, -1), ln_b.reshape(1, -1))


def _rope_tables(seq):
    pos = jnp.arange(seq, dtype=F32)

    def cos_sin(rot_dim, base):
        inv_freq = base ** (-jnp.arange(0, rot_dim, 2, dtype=F32) / rot_dim)
        ang = pos[:, None] * inv_freq[None, :]
        return jnp.cos(ang), jnp.sin(ang)

    c, s = cos_sin(RET_QK_DIM, RET_ROT_BASE)
    ret_tab = jnp.concatenate([c, c, -s, s], axis=-1)

    c, s = cos_sin(DIFF_ROT_DIM, ROPE_THETA)
    half = DIFF_ROT_DIM // 2
    rest = DIFF_QK_DIM - DIFF_ROT_DIM
    ones = jnp.ones((seq, rest), F32)
    zeros = jnp.zeros((seq, rest), F32)
    zh = jnp.zeros((seq, half), F32)
    diff_tab = jnp.concatenate([c, c, ones, zh, s, zeros, -s, zh, zeros], axis=-1)
    return ret_tab, diff_tab


def _prepare_weights(w_in, w_out, w_gate, w_up, w_down):
    wb = w_in.astype(BF16)
    c0 = 2 * RQ_COLS
    c1 = c0 + 2 * RV_COLS
    c2 = c1 + 2 * DQ_COLS
    pad = D_FF_PAD - D_FF
    zc = jnp.zeros((w_gate.shape[0], pad), BF16)
    wg = jnp.concatenate([w_gate.astype(BF16), zc], axis=1)
    wu = jnp.concatenate([w_up.astype(BF16), zc], axis=1)
    wd = jnp.concatenate([w_down.astype(BF16), jnp.zeros((pad, w_down.shape[1]), BF16)], axis=0)
    k, n = w_out.shape
    wo = w_out.astype(BF16).reshape(k, OUT_COL_GROUPS, n // OUT_COL_GROUPS).transpose(1, 0, 2)
    return wb[:, :c0], wb[:, c0:c1], wb[:, c1:c2], wb[:, c2:], wo, wg, wu, wd


def _encoder_layer(x, weights, tables, dec_f, dec_b, gn_w, lq1, lk1, lq2, lk2, subln_w,
                   ln1_g, ln1_b, ln2_g, ln2_b, lambda_init):
    batch, seq, d = x.shape
    w_rqk, w_rvg, w_dqk, w_dv, w_out, wg, wu, wd = weights
    x2 = x.reshape(batch * seq, d)
    ret_tab, diff_tab = tables

    xb = x2.astype(BF16)
    rvg = _project(xb, w_rvg, seq, "plain", RET_V_DIM)
    rqk = _project(xb, w_rqk, seq, "rope_full", RET_QK_DIM, ret_tab,
                   (1.0, RET_QK_DIM ** -0.5), RQ_COLS)
    dqk = _project(xb, w_dqk, seq, "rope_partial", DIFF_QK_DIM, diff_tab,
                   (DIFF_QK_DIM ** -0.5 * LOG2_E, 1.0), DQ_COLS)
    dvt = _project(xb, w_dv, seq, "plain_t", DIFF_V_DIM)

    ro = _retention(rqk, rvg, dec_f, dec_b, gn_w, batch, seq)
    do = _diff_attention(dqk, dvt, lq1, lk1, lq2, lk2, subln_w, batch, seq, lambda_init)

    x1, x1b = _out_projection(ro, do, w_out, x2, ln1_g, ln1_b)
    h = _gate_up(x1b, wg, wu)
    y = _down_projection(h, wd, x1, ln2_g, ln2_b)
    return y.reshape(batch, seq, d)


def kernel(x_prompt, x_sample, w_in, ret_decay_f, ret_decay_b, ret_gn_w, diff_lambda_q1,
           diff_lambda_k1, diff_lambda_q2, diff_lambda_k2, diff_subln_w, w_out, ln1_g, ln1_b,
           w_gate, w_up, w_down, ln2_g, ln2_b):
    y_prompt, y_sample = x_prompt, x_sample
    for l in range(DEPTH):
        lambda_init = 0.8 - 0.6 * math.exp(-0.3 * l)
        weights = _prepare_weights(w_in[l], w_out[l], w_gate[l], w_up[l], w_down[l])
        tables = _rope_tables(max(y_prompt.shape[1], y_sample.shape[1]))
        rest = (tables, ret_decay_f[l], ret_decay_b[l], ret_gn_w[l], diff_lambda_q1[l], diff_lambda_k1[l],
                diff_lambda_q2[l], diff_lambda_k2[l], diff_subln_w[l], ln1_g[l], ln1_b[l],
                ln2_g[l], ln2_b[l], lambda_init)
        y_prompt = _encoder_layer(y_prompt, weights, *rest)
        y_sample = _encoder_layer(y_sample, weights, *rest)
    return (y_prompt, y_sample)
```
